```python
import jax, jax.numpy as jnp
from jax import lax
import numpy as np

D_MODEL = 1024
BATCH = 16
SEQ = 2048
DEPTH = 4

N_MIXERS = 2
N_HEADS = 16
N_KV_HEADS = 4
HEAD_DIM = D_MODEL // N_HEADS
GROUP = N_HEADS // N_KV_HEADS
QKV_DIM = (N_HEADS + 2 * N_KV_HEADS) * HEAD_DIM
WINDOW = 128
BLOCK = 128
ROPE_THETA = 10000.0
CONV_WIDTH = 3
D_FF = 2816
N_SUBLAYERS = 3
N_ADA = 3 * N_SUBLAYERS
EPS = 1e-6
N_ATTN_LAYERS = (DEPTH + 1) // 2
N_CONV_LAYERS = DEPTH // 2

kernel_name = "hybrid_swa_sink_shortconv_macaron_adaln"


def rms_norm(x, gain):
    xf = x.astype(jnp.float32)
    y = xf * lax.rsqrt(jnp.mean(xf * xf, axis=-1, keepdims=True) + EPS)
    return (y * gain.astype(jnp.float32)).astype(x.dtype)


def modulate(h, shift, scale):
    return h * (1.0 + scale[:, None, :]) + shift[:, None, :]


def swiglu(h, w_up, w_down):
    gate, up = jnp.split(h @ w_up, 2, axis=-1)
    return (jax.nn.silu(gate) * up) @ w_down


def rope_tables(positions, dtype):
    inv_freq = ROPE_THETA ** (-jnp.arange(0, HEAD_DIM, 2, dtype=jnp.float32) / HEAD_DIM)
    ang = positions.astype(jnp.float32)[..., None] * inv_freq
    return jnp.cos(ang)[:, :, None, :].astype(dtype), jnp.sin(ang)[:, :, None, :].astype(dtype)


def apply_rope(t, cos, sin):
    t1, t2 = jnp.split(t, 2, axis=-1)
    return jnp.concatenate([t1 * cos - t2 * sin, t2 * cos + t1 * sin], axis=-1)


def swa_sink_attention(h, cos, sin, w_qkv, b_qkv, q_gain, k_gain, sinks, w_o, b_o):
    b, s, _ = h.shape
    nb = s // BLOCK
    qkv = h @ w_qkv + b_qkv
    q, k, v = jnp.split(qkv, [N_HEADS * HEAD_DIM, (N_HEADS + N_KV_HEADS) * HEAD_DIM], axis=-1)
    q = q.reshape(b, s, N_HEADS, HEAD_DIM)
    k = k.reshape(b, s, N_KV_HEADS, HEAD_DIM)
    v = v.reshape(b, s, N_KV_HEADS, HEAD_DIM)
    q = apply_rope(rms_norm(q, q_gain), cos, sin)
    k = apply_rope(rms_norm(k, k_gain), cos, sin)
    qb = q.reshape(b, nb, BLOCK, N_KV_HEADS, GROUP, HEAD_DIM)

    def banded(t):
        tb = t.reshape(b, nb, BLOCK, N_KV_HEADS, HEAD_DIM)
        prev = jnp.pad(tb[:, :-1], ((0, 0), (1, 0), (0, 0), (0, 0), (0, 0)))
        return jnp.concatenate([prev, tb], axis=2)

    kb, vb = banded(k), banded(v)
    scores = jnp.einsum('bnqkgd,bnskd->bnkgqs', qb, kb,
                        preferred_element_type=jnp.float32) * (HEAD_DIM ** -0.5)
    blk = jnp.arange(nb)[:, None, None] * BLOCK
    q_pos = blk + jnp.arange(BLOCK)[None, :, None]
    k_pos = blk - BLOCK + jnp.arange(2 * BLOCK)[None, None, :]
    rel = q_pos - k_pos
    mask = (rel >= 0) & (rel < WINDOW) & (k_pos >= 0)
    scores = jnp.where(mask[None, :, None, None], scores, -jnp.inf)
    sink = jnp.broadcast_to(sinks.astype(jnp.float32).reshape(1, 1, N_KV_HEADS, GROUP, 1, 1),
                            scores.shape[:-1] + (1,))
    probs = jax.nn.softmax(jnp.concatenate([scores, sink], axis=-1), axis=-1)[..., :-1]
    out = jnp.einsum('bnkgqs,bnskd->bnqkgd', probs.astype(vb.dtype), vb)
    return out.reshape(b, s, N_HEADS * HEAD_DIM) @ w_o + b_o


def short_gated_conv(h, w_in, conv_w, w_out):
    gb, gc, v = jnp.split(h @ w_in, 3, axis=-1)
    u = gc * v
    conv = lax.conv_general_dilated(
        u, conv_w[:, None, :], window_strides=(1,), padding=[(CONV_WIDTH - 1, 0)],
        dimension_numbers=('NWC', 'WIO', 'NWC'), feature_group_count=D_MODEL)
    return (gb * conv) @ w_out


def setup_inputs(seed: int = 0) -> dict:
    key = jax.random.key(seed)
    ks = jax.random.split(key, 20)
    f32 = jnp.float32
    d = D_MODEL
    nrm = lambda k, shape, fan_in: jax.random.normal(k, shape, f32) * (fan_in ** -0.5)
    x = jax.random.normal(ks[0], (BATCH, SEQ, d), f32)
    c = jax.random.normal(ks[1], (BATCH, d), f32)
    positions = (jnp.arange(SEQ, dtype=jnp.int32)[None, :]
                 + jax.random.randint(ks[2], (BATCH, 1), 0, 4096, dtype=jnp.int32))
    norm_gain = 1.0 + 0.05 * jax.random.normal(ks[3], (DEPTH, N_SUBLAYERS, d), f32)
    w_ada = nrm(ks[4], (DEPTH, d, N_ADA * d), d)
    b_ada = 0.02 * jax.random.normal(ks[5], (DEPTH, N_ADA * d), f32)
    w_ffn_up = nrm(ks[6], (DEPTH, 2, d, 2 * D_FF), d)
    w_ffn_down = nrm(ks[7], (DEPTH, 2, D_FF, d), D_FF)
    attn_w_qkv = nrm(ks[8], (N_ATTN_LAYERS, d, QKV_DIM), d)
    attn_b_qkv = 0.02 * jax.random.normal(ks[9], (N_ATTN_LAYERS, QKV_DIM), f32)
    attn_q_gain = 1.0 + 0.05 * jax.random.normal(ks[10], (N_ATTN_LAYERS, HEAD_DIM), f32)
    attn_k_gain = 1.0 + 0.05 * jax.random.normal(ks[11], (N_ATTN_LAYERS, HEAD_DIM), f32)
    attn_sinks = 0.5 * jax.random.normal(ks[12], (N_ATTN_LAYERS, N_HEADS), f32)
    attn_w_o = nrm(ks[13], (N_ATTN_LAYERS, N_HEADS * HEAD_DIM, d), N_HEADS * HEAD_DIM)
    attn_b_o = 0.02 * jax.random.normal(ks[14], (N_ATTN_LAYERS, d), f32)
    conv_w_in = nrm(ks[15], (N_CONV_LAYERS, d, 3 * d), d)
    conv_w = nrm(ks[16], (N_CONV_LAYERS, CONV_WIDTH, d), CONV_WIDTH)
    conv_w_out = nrm(ks[17], (N_CONV_LAYERS, d, d), d)
    return {"x": x, "c": c, "positions": positions, "norm_gain": norm_gain,
            "w_ada": w_ada, "b_ada": b_ada, "w_ffn_up": w_ffn_up, "w_ffn_down": w_ffn_down,
            "attn_w_qkv": attn_w_qkv, "attn_b_qkv": attn_b_qkv, "attn_q_gain": attn_q_gain,
            "attn_k_gain": attn_k_gain, "attn_sinks": attn_sinks, "attn_w_o": attn_w_o,
            "attn_b_o": attn_b_o, "conv_w_in": conv_w_in, "conv_w": conv_w,
            "conv_w_out": conv_w_out}


def reference(x, c, positions, norm_gain, w_ada, b_ada, w_ffn_up, w_ffn_down,
              attn_w_qkv, attn_b_qkv, attn_q_gain, attn_k_gain, attn_sinks, attn_w_o,
              attn_b_o, conv_w_in, conv_w, conv_w_out):
    cos, sin = rope_tables(positions, x.dtype)
    c_act = jax.nn.silu(c)
    for i in range(DEPTH):
        mod = c_act @ w_ada[i] + b_ada[i]
        sh1, sc1, g1, sh2, sc2, g2, sh3, sc3, g3 = jnp.split(mod, N_ADA, axis=-1)
        h = modulate(rms_norm(x, norm_gain[i, 0]), sh1, sc1)
        x = x + 0.5 * g1[:, None, :] * swiglu(h, w_ffn_up[i, 0], w_ffn_down[i, 0])
        h = modulate(rms_norm(x, norm_gain[i, 1]), sh2, sc2)
        j = i // N_MIXERS
        if i % N_MIXERS == 0:
            y = swa_sink_attention(h, cos, sin, attn_w_qkv[j], attn_b_qkv[j], attn_q_gain[j],
                                   attn_k_gain[j], attn_sinks[j], attn_w_o[j], attn_b_o[j])
        else:
            y = short_gated_conv(h, conv_w_in[j], conv_w[j], conv_w_out[j])
        x = x + g2[:, None, :] * y
        h = modulate(rms_norm(x, norm_gain[i, 2]), sh3, sc3)
        x = x + 0.5 * g3[:, None, :] * swiglu(h, w_ffn_up[i, 1], w_ffn_down[i, 1])
    return x
```

```python
import functools

import numpy as np
import jax
import jax.numpy as jnp
from jax import lax
from jax.experimental import pallas as pl
from jax.experimental.pallas import tpu as pltpu

F32 = jnp.float32
BF16 = jnp.bfloat16

EPS = 1e-6
ROPE_THETA = 10000.0
N_HEADS = 16
N_KV_HEADS = 4
HEAD_DIM = 64
GROUP = N_HEADS // N_KV_HEADS
BLOCK = 128
CONV_WIDTH = 3
N_SUBLAYERS = 3
LANES = 128
MXU_COLS = 256
FF_CHUNK = MXU_COLS
TOKEN_TILE = 512
ROPE_ROWS = 2048
ADA_COLS = 1536
VMEM_LIMIT = 56 * 1024 * 1024


def _const_spec(shape):
    nd = len(shape)
    return pl.BlockSpec(shape, lambda *_: (0,) * nd, pipeline_mode=pl.Buffered(1))


def _sigmoid(x):
    return 1.0 / (1.0 + jnp.exp(-x))


def _mod_norm(x, gain, shift, scale):
    ms = jnp.mean(x * x, axis=-1, keepdims=True)
    y = (x * lax.rsqrt(ms + EPS)) * gain
    return y * (1.0 + scale) + shift


def _ada_kernel(c_ref, w_ref, b_ref, o_ref):
    c = c_ref[...]
    ca = (c * _sigmoid(c)).astype(BF16)
    w = w_ref[0].astype(BF16)
    o_ref[0] = jnp.dot(ca, w, preferred_element_type=F32) + b_ref[0]


def _ada_call(c, w_ada, b_ada):
    depth, d, n = w_ada.shape
    b = c.shape[0]
    tn = ADA_COLS
    return pl.pallas_call(
        _ada_kernel,
        grid=(depth, n // tn),
        in_specs=[
            pl.BlockSpec((b, d), lambda l, j: (0, 0)),
            pl.BlockSpec((1, d, tn), lambda l, j: (l, 0, j)),
            pl.BlockSpec((1, 1, tn), lambda l, j: (l, 0, j)),
        ],
        out_specs=pl.BlockSpec((1, b, tn), lambda l, j: (l, 0, j)),
        out_shape=jax.ShapeDtypeStruct((depth, b, n), F32),
        compiler_params=pltpu.CompilerParams(
            dimension_semantics=("arbitrary", "arbitrary"),
            vmem_limit_bytes=VMEM_LIMIT),
        name="ada_mod",
    )(c, w_ada, b_ada.reshape(depth, 1, n))


def _rope_kernel(pos_ref, invf_ref, cos_ref, sin_ref):
    ang = pos_ref[...].astype(F32) * invf_ref[...]
    cos_ref[...] = jnp.cos(ang)
    sin_ref[...] = jnp.sin(ang)


def _rope_call(positions):
    b, s = positions.shape
    half = HEAD_DIM // 2
    per_row = LANES // half
    rows = b * s // per_row
    inv_freq = ROPE_THETA ** (-jnp.arange(0, HEAD_DIM, 2, dtype=F32) / HEAD_DIM)
    invf = jnp.tile(inv_freq, per_row).reshape(1, LANES)
    pos_rep = jnp.repeat(positions.reshape(rows, per_row), half, axis=1)
    tr = min(ROPE_ROWS, rows)
    cos_c, sin_c = pl.pallas_call(
        _rope_kernel,
        grid=(rows // tr,),
        in_specs=[pl.BlockSpec((tr, LANES), lambda i: (i, 0)),
                  pl.BlockSpec((1, LANES), lambda i: (0, 0))],
        out_specs=[pl.BlockSpec((tr, LANES), lambda i: (i, 0)),
                   pl.BlockSpec((tr, LANES), lambda i: (i, 0))],
        out_shape=[jax.ShapeDtypeStruct((rows, LANES), F32)] * 2,
        compiler_params=pltpu.CompilerParams(dimension_semantics=("arbitrary",)),
        name="rope_tables",
    )(pos_rep, invf)
    cos = cos_c.reshape(b, s, half)
    sin = sin_c.reshape(b, s, half)
    cos_t = jnp.concatenate([cos, cos, cos, cos], axis=-1)
    sin_t = jnp.concatenate([-sin, sin, -sin, sin], axis=-1)
    return cos_t, sin_t


def _ffn_kernel(x_ref, mod_ref, gain_ref, wup_ref, wdn_ref, o_ref, a_scr, *, sub):
    x = x_ref[0]
    shift = mod_ref[0, 0, 3 * sub + 0:3 * sub + 1, :]
    scale = mod_ref[0, 0, 3 * sub + 1:3 * sub + 2, :]
    gate = mod_ref[0, 0, 3 * sub + 2:3 * sub + 3, :]
    h = _mod_norm(x, gain_ref[...], shift, scale).astype(BF16)
    n_chunks = wup_ref.shape[0]
    for c in range(n_chunks):
        gu = jnp.dot(h, wup_ref[c], preferred_element_type=F32)
        g = gu[:, :FF_CHUNK]
        u = gu[:, FF_CHUNK:]
        a_scr[:, c * FF_CHUNK:(c + 1) * FF_CHUNK] = ((g * _sigmoid(g)) * u).astype(BF16)
    y = jnp.dot(a_scr[...], wdn_ref[...], preferred_element_type=F32)
    o_ref[0] = x + (0.5 * gate) * y


def _ffn_call(x, mod, gain, wup, wdn, layer, sub):
    b, s, d = x.shape
    ts = TOKEN_TILE
    d_ff = wdn.shape[0]
    return pl.pallas_call(
        functools.partial(_ffn_kernel, sub=sub),
        grid=(b, s // ts),
        in_specs=[
            pl.BlockSpec((1, ts, d), lambda i, j: (i, j, 0)),
            pl.BlockSpec((1, 1) + mod.shape[2:], lambda i, j: (layer, i, 0, 0)),
            _const_spec(gain.shape),
            _const_spec(wup.shape),
            _const_spec(wdn.shape),
        ],
        out_specs=pl.BlockSpec((1, ts, d), lambda i, j: (i, j, 0)),
        out_shape=jax.ShapeDtypeStruct(x.shape, F32),
        scratch_shapes=[pltpu.VMEM((ts, d_ff), BF16)],
        compiler_params=pltpu.CompilerParams(
            dimension_semantics=("arbitrary", "arbitrary"),
            vmem_limit_bytes=VMEM_LIMIT),
        name=f"ffn_l{layer}_s{sub}",
    )(x, mod, gain, wup, wdn)


def _group_sumsq(t):
    sq = t * t
    hi = sq.astype(BF16)
    lo = (sq - hi.astype(F32)).astype(BF16)
    r = lax.broadcasted_iota(jnp.int32, (MXU_COLS, MXU_COLS), 0) // HEAD_DIM
    c = lax.broadcasted_iota(jnp.int32, (MXU_COLS, MXU_COLS), 1) // HEAD_DIM
    ones = jnp.where(r == c, 1.0, 0.0).astype(BF16)
    return (jnp.dot(hi, ones, preferred_element_type=F32)
            + jnp.dot(lo, ones, preferred_element_type=F32))


def _head_norm_rope(t, gain, cos, sin, upper):
    ms = _group_sumsq(t) * (1.0 / HEAD_DIM)
    tn = (t * lax.rsqrt(ms + EPS)) * gain
    outs = []
    for j in range(t.shape[1] // LANES):
        v = tn[:, j * LANES:(j + 1) * LANES]
        fwd = pltpu.roll(v, HEAD_DIM // 2, 1)
        bwd = pltpu.roll(v, LANES - HEAD_DIM // 2, 1)
        swap = jnp.where(upper, fwd, bwd)
        outs.append(v * cos + swap * sin)
    return outs


def _attn_kernel(x_ref, mod_ref, gain_ref, cos_ref, sin_ref, wqkv_ref, bqkv_ref,
                 qg_ref, kg_ref, sink_ref, wo_ref, bo_ref, o_ref,
                 q_scr, k_scr, v_scr, att_scr):
    tq = x_ref.shape[1]
    d = x_ref.shape[2]
    nb = tq // BLOCK
    n_q_slices = d // LANES
    n_kv_slices = (N_KV_HEADS * HEAD_DIM) // LANES
    first = pl.program_id(1) == 0

    @pl.when(first)
    def _():
        k_scr[0:BLOCK, :] = jnp.zeros((BLOCK, k_scr.shape[1]), BF16)
        v_scr[0:BLOCK, :] = jnp.zeros((BLOCK, v_scr.shape[1]), BF16)

    x = x_ref[0]
    shift = mod_ref[0, 0, 3:4, :]
    scale = mod_ref[0, 0, 4:5, :]
    gate = mod_ref[0, 0, 5:6, :]
    h = _mod_norm(x, gain_ref[...], shift, scale).astype(BF16)

    cos = cos_ref[0]
    sin = sin_ref[0]
    lane = lax.broadcasted_iota(jnp.int32, (tq, LANES), 1)
    upper = (lane % HEAD_DIM) >= (HEAD_DIM // 2)

    qgain = qg_ref[...] * (HEAD_DIM ** -0.5)
    for c in range(d // MXU_COLS):
        cols = slice(c * MXU_COLS, (c + 1) * MXU_COLS)
        t = jnp.dot(h, wqkv_ref[:, cols], preferred_element_type=F32) + bqkv_ref[:, cols]
        outs = _head_norm_rope(t, qgain, cos, sin, upper)
        for j, o in enumerate(outs):
            sl = c * (MXU_COLS // LANES) + j
            q_scr[sl] = o.astype(BF16)
    kcols = slice(d, d + N_KV_HEADS * HEAD_DIM)
    vcols = slice(d + N_KV_HEADS * HEAD_DIM, d + 2 * N_KV_HEADS * HEAD_DIM)
    t = jnp.dot(h, wqkv_ref[:, kcols], preferred_element_type=F32) + bqkv_ref[:, kcols]
    outs = _head_norm_rope(t, kg_ref[...], cos, sin, upper)
    for j, o in enumerate(outs):
        k_scr[BLOCK:, j * LANES:(j + 1) * LANES] = o.astype(BF16)
    t = jnp.dot(h, wqkv_ref[:, vcols], preferred_element_type=F32) + bqkv_ref[:, vcols]
    v_scr[BLOCK:, :] = t.astype(BF16)

    qi = lax.broadcasted_iota(jnp.int32, (BLOCK, 2 * BLOCK), 0)
    kj = lax.broadcasted_iota(jnp.int32, (BLOCK, 2 * BLOCK), 1)
    band = (kj > qi) & (kj <= qi + BLOCK)
    band_first = band & (kj >= jnp.where(first, BLOCK, 0))
    low = lax.broadcasted_iota(jnp.int32, (2 * BLOCK, LANES), 1) < HEAD_DIM
    zero = jnp.zeros((2 * BLOCK, LANES), BF16)

    per_grp = n_q_slices // n_kv_slices
    for n in range(nb):
        mask = band_first if n == 0 else band
        rows = slice(n * BLOCK, n * BLOCK + 2 * BLOCK)
        for sg in range(n_kv_slices):
            kb = k_scr[rows, sg * LANES:(sg + 1) * LANES]
            vb = v_scr[rows, sg * LANES:(sg + 1) * LANES]
            k_lo = jnp.where(low, kb, zero)
            k_hi = jnp.where(low, zero, kb)
            v_lo = jnp.where(low, vb, zero)
            v_hi = jnp.where(low, zero, vb)
            for g in range(per_grp):
                sl = sg * per_grp + g
                q = q_scr[sl, n * BLOCK:(n + 1) * BLOCK, :]
                acc = None
                for half, (kk, vv) in enumerate(((k_lo, v_lo), (k_hi, v_hi))):
                    sc = lax.dot_general(q, kk, (((1,), (1,)), ((), ())),
                                         preferred_element_type=F32)
                    sc = jnp.where(mask, sc, -jnp.inf)
                    sink = sink_ref[2 * sl + half]
                    m = jnp.maximum(jnp.max(sc, axis=-1, keepdims=True), sink)
                    p = jnp.exp(sc - m)
                    den = jnp.sum(p, axis=-1, keepdims=True) + jnp.exp(sink - m)
                    pn = (p * (1.0 / den)).astype(BF16)
                    o = jnp.dot(pn, vv, preferred_element_type=F32)
                    acc = o if acc is None else acc + o
                att_scr[n * BLOCK:(n + 1) * BLOCK, sl * LANES:(sl + 1) * LANES] = acc.astype(BF16)

    k_scr[0:BLOCK, :] = k_scr[tq:tq + BLOCK, :]
    v_scr[0:BLOCK, :] = v_scr[tq:tq + BLOCK, :]

    y = jnp.dot(att_scr[...], wo_ref[...], preferred_element_type=F32) + bo_ref[...]
    o_ref[0] = x + gate * y


def _q_perm():
    perm = []
    heads = []
    n_slices = N_HEADS * HEAD_DIM // LANES
    per_grp = GROUP
    for sl in range(n_slices):
        pair, g = divmod(sl, per_grp)
        for half in range(2):
            head = (2 * pair + half) * GROUP + g
            heads.append(head)
            perm.extend(range(head * HEAD_DIM, (head + 1) * HEAD_DIM))
    return np.asarray(perm, np.int32), np.asarray(heads, np.int32)


def _attn_call(x, mod, gain, cos_t, sin_t, wqkv, bqkv, qg, kg, sinks, wo, bo, layer):
    b, s, d = x.shape
    tq = TOKEN_TILE
    kvw = N_KV_HEADS * HEAD_DIM
    smem = pl.BlockSpec(memory_space=pltpu.SMEM)
    return pl.pallas_call(
        _attn_kernel,
        grid=(b, s // tq),
        in_specs=[
            pl.BlockSpec((1, tq, d), lambda i, j: (i, j, 0)),
            pl.BlockSpec((1, 1) + mod.shape[2:], lambda i, j: (layer, i, 0, 0)),
            _const_spec(gain.shape),
            pl.BlockSpec((1, tq, LANES), lambda i, j: (i, j, 0)),
            pl.BlockSpec((1, tq, LANES), lambda i, j: (i, j, 0)),
            _const_spec(wqkv.shape),
            _const_spec(bqkv.shape),
            _const_spec(qg.shape),
            _const_spec(kg.shape),
            smem,
            _const_spec(wo.shape),
            _const_spec(bo.shape),
        ],
        out_specs=pl.BlockSpec((1, tq, d), lambda i, j: (i, j, 0)),
        out_shape=jax.ShapeDtypeStruct(x.shape, F32),
        scratch_shapes=[
            pltpu.VMEM((d // LANES, tq, LANES), BF16),
            pltpu.VMEM((BLOCK + tq, kvw), BF16),
            pltpu.VMEM((BLOCK + tq, kvw), BF16),
            pltpu.VMEM((tq, d), BF16),
        ],
        compiler_params=pltpu.CompilerParams(
            dimension_semantics=("arbitrary", "arbitrary"),
            vmem_limit_bytes=VMEM_LIMIT),
        name=f"attn_l{layer}",
    )(x, mod, gain, cos_t, sin_t, wqkv, bqkv, qg, kg, sinks, wo, bo)


def _conv_kernel(x_ref, mod_ref, gain_ref, win_ref, cw_ref, wout_ref, o_ref, u_scr):
    tq = x_ref.shape[1]
    d = x_ref.shape[2]
    pad = u_scr.shape[0] - tq

    @pl.when(pl.program_id(1) == 0)
    def _():
        u_scr[0:pad, :] = jnp.zeros((pad, d), F32)

    x = x_ref[0]
    shift = mod_ref[0, 0, 3:4, :]
    scale = mod_ref[0, 0, 4:5, :]
    gate = mod_ref[0, 0, 5:6, :]
    h = _mod_norm(x, gain_ref[...], shift, scale).astype(BF16)
    gc = jnp.dot(h, win_ref[:, d:2 * d], preferred_element_type=F32)
    v = jnp.dot(h, win_ref[:, 2 * d:3 * d], preferred_element_type=F32)
    u_scr[pad:, :] = gc * v
    conv = cw_ref[CONV_WIDTH - 1:CONV_WIDTH, :] * u_scr[pad:, :]
    for k in range(1, CONV_WIDTH):
        w = cw_ref[CONV_WIDTH - 1 - k:CONV_WIDTH - k, :]
        conv = conv + w * u_scr[pad - k:pad - k + tq, :]
    u_scr[0:pad, :] = u_scr[tq:tq + pad, :]
    gb = jnp.dot(h, win_ref[:, 0:d], preferred_element_type=F32)
    y = jnp.dot((gb * conv).astype(BF16), wout_ref[...], preferred_element_type=F32)
    o_ref[0] = x + gate * y


def _conv_call(x, mod, gain, win, cw, wout, layer):
    b, s, d = x.shape
    tq = TOKEN_TILE
    return pl.pallas_call(
        _conv_kernel,
        grid=(b, s // tq),
        in_specs=[
            pl.BlockSpec((1, tq, d), lambda i, j: (i, j, 0)),
            pl.BlockSpec((1, 1) + mod.shape[2:], lambda i, j: (layer, i, 0, 0)),
            _const_spec(gain.shape),
            _const_spec(win.shape),
            _const_spec(cw.shape),
            _const_spec(wout.shape),
        ],
        out_specs=pl.BlockSpec((1, tq, d), lambda i, j: (i, j, 0)),
        out_shape=jax.ShapeDtypeStruct(x.shape, F32),
        scratch_shapes=[pltpu.VMEM((tq + 8, d), F32)],
        compiler_params=pltpu.CompilerParams(
            dimension_semantics=("arbitrary", "arbitrary"),
            vmem_limit_bytes=VMEM_LIMIT),
        name=f"conv_l{layer}",
    )(x, mod, gain, win, cw, wout)


def _prep_ffn(w_up, w_down):
    d, two_ff = w_up.shape
    d_ff = two_ff // 2
    n = d_ff // FF_CHUNK
    wg = w_up[:, :d_ff].reshape(d, n, FF_CHUNK)
    wu = w_up[:, d_ff:].reshape(d, n, FF_CHUNK)
    wup = jnp.concatenate([wg, wu], axis=-1).transpose(1, 0, 2).astype(BF16)
    return wup, w_down.astype(BF16)


def kernel(x, c, positions, norm_gain, w_ada, b_ada, w_ffn_up, w_ffn_down, attn_w_qkv,
           attn_b_qkv, attn_q_gain, attn_k_gain, attn_sinks, attn_w_o, attn_b_o,
           conv_w_in, conv_w, conv_w_out):
    depth = w_ada.shape[0]
    b, s, d = x.shape
    n_ada = w_ada.shape[2] // d
    mod = _ada_call(c, w_ada, b_ada).reshape(depth, b, n_ada, d)
    cos_t, sin_t = _rope_call(positions)
    perm, heads = _q_perm()
    reps_q = MXU_COLS // HEAD_DIM
    for i in range(depth):
        gains = norm_gain[i].reshape(N_SUBLAYERS, 1, d)
        wup, wdn = _prep_ffn(w_ffn_up[i, 0], w_ffn_down[i, 0])
        x = _ffn_call(x, mod, gains[0], wup, wdn, i, 0)
        j = i // 2
        if i % 2 == 0:
            wq = attn_w_qkv[j][:, :d][:, perm]
            wqkv = jnp.concatenate([wq, attn_w_qkv[j][:, d:]], axis=1).astype(BF16)
            bq = attn_b_qkv[j][:d][perm]
            bqkv = jnp.concatenate([bq, attn_b_qkv[j][d:]]).reshape(1, -1)
            qg = jnp.tile(attn_q_gain[j], reps_q).reshape(1, MXU_COLS)
            kg = jnp.tile(attn_k_gain[j], reps_q).reshape(1, MXU_COLS)
            x = _attn_call(x, mod, gains[1], cos_t, sin_t, wqkv, bqkv, qg, kg,
                           attn_sinks[j][heads], attn_w_o[j][perm, :].astype(BF16),
                           attn_b_o[j].reshape(1, d), i)
        else:
            x = _conv_call(x, mod, gains[1], conv_w_in[j].astype(BF16), conv_w[j],
                           conv_w_out[j].astype(BF16), i)
        wup, wdn = _prep_ffn(w_ffn_up[i, 1], w_ffn_down[i, 1])
        x = _ffn_call(x, mod, gains[2], wup, wdn, i, 2)
    return x
```

```python
import functools
import math

import numpy as np
import jax
import jax.numpy as jnp
from jax import lax
from jax.experimental import pallas as pl
from jax.experimental.pallas import tpu as pltpu

F32 = jnp.float32
BF16 = jnp.bfloat16

EPS = 1e-6
ROPE_THETA = 10000.0
N_HEADS = 16
N_KV_HEADS = 4
HEAD_DIM = 64
GROUP = N_HEADS // N_KV_HEADS
BLOCK = 128
CONV_WIDTH = 3
LANES = 128
MXU_COLS = 256
FF_CHUNK = MXU_COLS
TOKEN_TILE = 512
FFN_TILE = 1024
ATTN_TILE = 1024
ROPE_ROWS = 2048
ADA_COLS = 1536
VMEM_LIMIT = 56 * 1024 * 1024
LOG2E = math.log2(math.e)
KV_WIDTH = N_KV_HEADS * HEAD_DIM
Q_SLICES = N_HEADS * HEAD_DIM // LANES
KV_SLICES = KV_WIDTH // LANES
STACK = Q_SLICES // KV_SLICES


def _resident(shape, index):
    full = tuple(index) + (0,) * (len(shape) - len(index))
    return pl.BlockSpec(shape, lambda *_: full, pipeline_mode=pl.Buffered(1))


def _sigmoid(x):
    return 1.0 / (1.0 + jnp.exp(-x))


def _mod_norm(x, gain, shift, scale):
    ms = jnp.mean(x * x, axis=-1, keepdims=True)
    y = (x * lax.rsqrt(ms + EPS)) * gain
    return y * (1.0 + scale) + shift


def _mod_rows(mod_ref, sub):
    shift = mod_ref[3 * sub + 0:3 * sub + 1, :]
    scale = mod_ref[3 * sub + 1:3 * sub + 2, :]
    gate = mod_ref[3 * sub + 2:3 * sub + 3, :]
    return shift, scale, gate


def _token_specs(x, mod, layer, tile):
    _, _, d = x.shape
    x_spec = pl.BlockSpec((None, tile, d), lambda i, j: (i, j, 0))
    mod_spec = pl.BlockSpec((None, None) + mod.shape[2:], lambda i, j: (layer, i, 0, 0))
    return x_spec, mod_spec


_GRID_PARAMS = pltpu.CompilerParams(
    dimension_semantics=("arbitrary", "arbitrary"), vmem_limit_bytes=VMEM_LIMIT)


def _ada_kernel(c_ref, w_ref, b_ref, o_ref):
    c = c_ref[...]
    ca = (c * _sigmoid(c)).astype(BF16)
    o_ref[...] = jnp.dot(ca, w_ref[...].astype(BF16), preferred_element_type=F32) + b_ref[...]


def _ada_call(c, w_ada, b_ada):
    depth, d, n = w_ada.shape
    b = c.shape[0]
    tn = ADA_COLS
    return pl.pallas_call(
        _ada_kernel,
        grid=(depth, n // tn),
        in_specs=[
            pl.BlockSpec((b, d), lambda l, j: (0, 0)),
            pl.BlockSpec((None, d, tn), lambda l, j: (l, 0, j)),
            pl.BlockSpec((None, 1, tn), lambda l, j: (l, 0, j)),
        ],
        out_specs=pl.BlockSpec((None, b, tn), lambda l, j: (l, 0, j)),
        out_shape=jax.ShapeDtypeStruct((depth, b, n), F32),
        compiler_params=_GRID_PARAMS,
        name="ada_mod",
    )(c, w_ada, b_ada.reshape(depth, 1, n))


def _rope_kernel(pos_ref, invf_ref, cos_ref, sin_ref):
    ang = pos_ref[...].astype(F32) * invf_ref[...]
    cos_ref[...] = jnp.cos(ang)
    sin_ref[...] = jnp.sin(ang)


def _rope_call(positions):
    b, s = positions.shape
    half = HEAD_DIM // 2
    per_row = LANES // half
    rows = b * s // per_row
    inv_freq = ROPE_THETA ** (-jnp.arange(0, HEAD_DIM, 2, dtype=F32) / HEAD_DIM)
    invf = jnp.tile(inv_freq, per_row).reshape(1, LANES)
    pos_rep = jnp.repeat(positions.reshape(rows, per_row), half, axis=1)
    tr = min(ROPE_ROWS, rows)
    cos_c, sin_c = pl.pallas_call(
        _rope_kernel,
        grid=(rows // tr,),
        in_specs=[pl.BlockSpec((tr, LANES), lambda i: (i, 0)),
                  pl.BlockSpec((1, LANES), lambda i: (0, 0))],
        out_specs=[pl.BlockSpec((tr, LANES), lambda i: (i, 0)),
                   pl.BlockSpec((tr, LANES), lambda i: (i, 0))],
        out_shape=[jax.ShapeDtypeStruct((rows, LANES), F32)] * 2,
        compiler_params=pltpu.CompilerParams(dimension_semantics=("arbitrary",)),
        name="rope_tables",
    )(pos_rep, invf)
    cos = cos_c.reshape(b, s, half)
    sin = sin_c.reshape(b, s, half)
    cos_t = jnp.concatenate([cos, cos, cos, cos], axis=-1)
    sin_t = jnp.concatenate([-sin, -sin, sin, sin], axis=-1)
    return cos_t, sin_t


def _ffn_kernel(x_ref, mod_ref, gain_ref, wup_ref, wdn_ref, o_ref, a_scr, *, sub):
    x = x_ref[...]
    shift, scale, gate = _mod_rows(mod_ref, sub)
    h = _mod_norm(x, gain_ref[...], shift, scale).astype(BF16)
    d_ff = wdn_ref.shape[0]
    for c in range(d_ff // FF_CHUNK):
        lo = c * FF_CHUNK
        g = jnp.dot(h, wup_ref[:, lo:lo + FF_CHUNK], preferred_element_type=F32)
        u = jnp.dot(h, wup_ref[:, d_ff + lo:d_ff + lo + FF_CHUNK], preferred_element_type=F32)
        a_scr[:, lo:lo + FF_CHUNK] = ((g * _sigmoid(g)) * u).astype(BF16)
    y = jnp.dot(a_scr[...], wdn_ref[...], preferred_element_type=F32)
    o_ref[...] = x + (0.5 * gate) * y


def _ffn_call(x, mod, gains, wup, wdn, layer, sub):
    b, s, d = x.shape
    ts = FFN_TILE
    d_ff = wdn.shape[2]
    which = sub // 2
    x_spec, mod_spec = _token_specs(x, mod, layer, ts)
    return pl.pallas_call(
        functools.partial(_ffn_kernel, sub=sub),
        grid=(b, s // ts),
        in_specs=[
            x_spec, mod_spec,
            _resident((None, None, 1, d), (layer, sub)),
            _resident((None, None, d, 2 * d_ff), (layer, which)),
            _resident((None, None, d_ff, d), (layer, which)),
        ],
        out_specs=x_spec,
        out_shape=jax.ShapeDtypeStruct(x.shape, F32),
        scratch_shapes=[pltpu.VMEM((ts, d_ff), BF16)],
        compiler_params=_GRID_PARAMS,
        name=f"ffn_l{layer}_s{sub}",
    )(x, mod, gains, wup, wdn)


def _norm_rope(t, ms, gain, cos, sin):
    tn = (t * lax.rsqrt(ms + EPS)) * gain
    outs = []
    for j in range(t.shape[1] // LANES):
        v = tn[:, j * LANES:(j + 1) * LANES]
        outs.append(v * cos + pltpu.roll(v, LANES // 2, 1) * sin)
    return outs


def _attn_kernel(x_ref, mod_ref, gain_ref, cos_ref, sin_ref, wqkv_ref, bqkv_ref,
                 qg_ref, kg_ref, sink_ref, wo_ref, bo_ref, o_ref,
                 q_scr, k_scr, v_scr, att_scr, sink_scr, sc_scr, *, attn_layer):
    tq, d = x_ref.shape
    nb = tq // BLOCK
    rows_stack = STACK * BLOCK
    first = pl.program_id(1) == 0

    @pl.when(first)
    def _():
        k_scr[0:BLOCK, :] = jnp.zeros((BLOCK, KV_WIDTH), BF16)
        v_scr[0:BLOCK, :] = jnp.zeros((BLOCK, KV_WIDTH), BF16)

    @pl.when(first & (pl.program_id(0) == 0))
    def _():
        for sg in range(KV_SLICES):
            for half in range(2):
                for g in range(STACK):
                    val = sink_ref[attn_layer, 2 * (sg * STACK + g) + half] * LOG2E
                    sink_scr[2 * sg + half, g * BLOCK:(g + 1) * BLOCK, :] = jnp.full(
                        (BLOCK, LANES), val, F32)

    x = x_ref[...]
    shift, scale, gate = _mod_rows(mod_ref, 1)
    h = _mod_norm(x, gain_ref[...], shift, scale).astype(BF16)

    cos = cos_ref[...]
    sin = sin_ref[...]

    def head_of(col):
        return (col // LANES) * 2 + (col // (HEAD_DIM // 2)) % 2

    r = head_of(lax.broadcasted_iota(jnp.int32, (MXU_COLS, MXU_COLS), 0))
    c = head_of(lax.broadcasted_iota(jnp.int32, (MXU_COLS, MXU_COLS), 1))
    group_mean = jnp.where(r == c, 1.0 / HEAD_DIM, 0.0).astype(BF16)

    n_chunks = (d + 2 * KV_WIDTH) // MXU_COLS
    n_norm = (d + KV_WIDTH) // MXU_COLS
    qgain = qg_ref[...] * (HEAD_DIM ** -0.5 * LOG2E)

    def project(cc):
        cols = slice(cc * MXU_COLS, (cc + 1) * MXU_COLS)
        return jnp.dot(h, wqkv_ref[:, cols], preferred_element_type=F32) + bqkv_ref[:, cols]

    def finish(cc, t):
        if cc == n_norm:
            v_scr[BLOCK:, :] = t.astype(BF16)
            return
        ms = jnp.dot((t * t).astype(BF16), group_mean, preferred_element_type=F32)
        is_q = cc < d // MXU_COLS
        outs = _norm_rope(t, ms, qgain if is_q else kg_ref[...], cos, sin)
        for j, o in enumerate(outs):
            ob = o.astype(BF16)
            if is_q:
                sg, g = divmod(cc * (MXU_COLS // LANES) + j, STACK)
                for n in range(nb):
                    q_scr[sg, n, g * BLOCK:(g + 1) * BLOCK, :] = ob[n * BLOCK:(n + 1) * BLOCK]
            else:
                k_scr[BLOCK:, j * LANES:(j + 1) * LANES] = ob

    ahead = project(0)
    for cc in range(n_chunks):
        t = ahead
        if cc + 1 < n_chunks:
            ahead = project(cc + 1)
        finish(cc, t)

    qi = lax.broadcasted_iota(jnp.int32, (rows_stack, LANES), 0) % BLOCK
    kc = lax.broadcasted_iota(jnp.int32, (rows_stack, LANES), 1)
    take_prev = kc > qi
    low_q = kc < HEAD_DIM
    kv_lane = lax.broadcasted_iota(jnp.int32, (2 * BLOCK, LANES), 1)
    k_first = (kv_lane // (HEAD_DIM // 2)) % 2 == 0
    v_first = kv_lane < HEAD_DIM
    no_prev = jnp.where(first, -jnp.inf, 0.0)
    zero_k = jnp.zeros((2 * BLOCK, LANES), BF16)
    one_k = jnp.ones((2 * BLOCK, LANES), BF16)
    zero_p = jnp.zeros((rows_stack, LANES), F32)

    def block_rows(n, count):
        start = n * BLOCK if isinstance(n, int) else pl.multiple_of(n * BLOCK, BLOCK)
        return pl.ds(start, count * BLOCK)

    def scores(n):
        slot = n % sc_scr.shape[0]
        for sg in range(KV_SLICES):
            kb = k_scr[block_rows(n, 2), sg * LANES:(sg + 1) * LANES]
            q = q_scr[sg, n]
            for half in range(2):
                kk = jnp.where(k_first, kb, zero_k) if half == 0 else jnp.where(k_first, zero_k, kb)
                sc_scr[slot, 2 * sg + half] = lax.dot_general(
                    q, kk, (((1,), (1,)), ((), ())), preferred_element_type=F32)

    def attend(n):
        slot = n % sc_scr.shape[0]
        bias = jnp.where(n == 0, no_prev, 0.0)
        for sg in range(KV_SLICES):
            vb = v_scr[block_rows(n, 2), sg * LANES:(sg + 1) * LANES]
            res = []
            for half in range(2):
                s = jnp.where(take_prev, sc_scr[slot, 2 * sg + half, :, :BLOCK] + bias,
                              sc_scr[slot, 2 * sg + half, :, BLOCK:])
                sink = sink_scr[2 * sg + half]
                m = jnp.maximum(jnp.max(s, axis=-1, keepdims=True), sink)
                p = jnp.exp2(s - m)
                p_cat = jnp.concatenate([jnp.where(take_prev, p, zero_p).astype(BF16),
                                         jnp.where(take_prev, zero_p, p).astype(BF16)], axis=1)
                vv = jnp.where(v_first, vb, one_k) if half == 0 else jnp.where(v_first, one_k, vb)
                o = jnp.dot(p_cat, vv, preferred_element_type=F32)
                res.append((o, o + jnp.exp2(sink - m)))
            (o_lo, d_lo), (o_hi, d_hi) = res
            numer = jnp.where(low_q, o_lo, o_hi)
            den = pltpu.roll(jnp.where(low_q, d_hi, d_lo), HEAD_DIM, 1)
            out = (numer * (1.0 / den)).astype(BF16)
            for g in range(STACK):
                sl = sg * STACK + g
                att_scr[block_rows(n, 1), sl * LANES:(sl + 1) * LANES] = (
                    out[g * BLOCK:(g + 1) * BLOCK])

    scores(0)

    def trip(n, carry):
        attend(n)
        scores(n + 1)
        return carry

    lax.fori_loop(0, nb - 1, trip, 0)
    attend(nb - 1)

    k_scr[0:BLOCK, :] = k_scr[tq:tq + BLOCK, :]
    v_scr[0:BLOCK, :] = v_scr[tq:tq + BLOCK, :]

    y = jnp.dot(att_scr[...], wo_ref[...], preferred_element_type=F32) + bo_ref[...]
    o_ref[...] = x + gate * y


def _slice_cols(head_a, head_b):
    half = HEAD_DIM // 2
    cols = []
    for part in range(2):
        for head in (head_a, head_b):
            start = head * HEAD_DIM + part * half
            cols.extend(range(start, start + half))
    return cols


def _attn_layout():
    q_cols, k_cols, o_rows, heads = [], [], [], []
    for sl in range(Q_SLICES):
        pair, g = divmod(sl, STACK)
        a, b = (2 * pair) * GROUP + g, (2 * pair + 1) * GROUP + g
        heads.extend([a, b])
        q_cols.extend(_slice_cols(a, b))
        o_rows.extend(range(a * HEAD_DIM, (a + 1) * HEAD_DIM))
        o_rows.extend(range(b * HEAD_DIM, (b + 1) * HEAD_DIM))
    for sg in range(KV_SLICES):
        k_cols.extend(_slice_cols(2 * sg, 2 * sg + 1))
    as_idx = lambda v: np.asarray(v, np.int32)
    return as_idx(q_cols), as_idx(k_cols), as_idx(o_rows), as_idx(heads)


def _slice_gain(gain):
    half = HEAD_DIM // 2
    lo, hi = gain[:, :half], gain[:, half:]
    one_slice = jnp.concatenate([lo, lo, hi, hi], axis=1)
    reps = MXU_COLS // LANES
    return jnp.tile(one_slice, (1, reps)).reshape(gain.shape[0], 1, MXU_COLS)


def _attn_call(x, mod, gains, cos_t, sin_t, wqkv, bqkv, qg, kg, sinks, wo, bo, layer, j):
    b, s, d = x.shape
    tq = ATTN_TILE
    nb = tq // BLOCK
    x_spec, mod_spec = _token_specs(x, mod, layer, tq)
    rope_spec = pl.BlockSpec((None, tq, LANES), lambda i, k: (i, k, 0))
    return pl.pallas_call(
        functools.partial(_attn_kernel, attn_layer=j),
        grid=(b, s // tq),
        in_specs=[
            x_spec, mod_spec,
            _resident((None, None, 1, d), (layer, 1)),
            rope_spec, rope_spec,
            _resident((None,) + wqkv.shape[1:], (j,)),
            _resident((None,) + bqkv.shape[1:], (j,)),
            _resident((None,) + qg.shape[1:], (j,)),
            _resident((None,) + kg.shape[1:], (j,)),
            pl.BlockSpec(memory_space=pltpu.SMEM),
            _resident((None,) + wo.shape[1:], (j,)),
            _resident((None,) + bo.shape[1:], (j,)),
        ],
        out_specs=x_spec,
        out_shape=jax.ShapeDtypeStruct(x.shape, F32),
        scratch_shapes=[
            pltpu.VMEM((KV_SLICES, nb, STACK * BLOCK, LANES), BF16),
            pltpu.VMEM((BLOCK + tq, KV_WIDTH), BF16),
            pltpu.VMEM((BLOCK + tq, KV_WIDTH), BF16),
            pltpu.VMEM((tq, d), BF16),
            pltpu.VMEM((2 * KV_SLICES, STACK * BLOCK, LANES), F32),
            pltpu.VMEM((2, 2 * KV_SLICES, STACK * BLOCK, 2 * BLOCK), F32),
        ],
        compiler_params=_GRID_PARAMS,
        name=f"attn_l{layer}",
    )(x, mod, gains, cos_t, sin_t, wqkv, bqkv, qg, kg, sinks, wo, bo)


def _conv_kernel(x_ref, mod_ref, gain_ref, win_ref, cw_ref, wout_ref, o_ref, u_scr):
    tq, d = x_ref.shape
    pad = u_scr.shape[0] - tq

    @pl.when(pl.program_id(1) == 0)
    def _():
        u_scr[0:pad, :] = jnp.zeros((pad, d), F32)

    x = x_ref[...]
    shift, scale, gate = _mod_rows(mod_ref, 1)
    h = _mod_norm(x, gain_ref[...], shift, scale).astype(BF16)
    gc = jnp.dot(h, win_ref[:, d:2 * d], preferred_element_type=F32)
    v = jnp.dot(h, win_ref[:, 2 * d:3 * d], preferred_element_type=F32)
    u_scr[pad:, :] = gc * v
    conv = cw_ref[CONV_WIDTH - 1:CONV_WIDTH, :] * u_scr[pad:, :]
    for k in range(1, CONV_WIDTH):
        w = cw_ref[CONV_WIDTH - 1 - k:CONV_WIDTH - k, :]
        conv = conv + w * u_scr[pad - k:pad - k + tq, :]
    u_scr[0:pad, :] = u_scr[tq:tq + pad, :]
    gb = jnp.dot(h, win_ref[:, 0:d], preferred_element_type=F32)
    y = jnp.dot((gb * conv).astype(BF16), wout_ref[...], preferred_element_type=F32)
    o_ref[...] = x + gate * y


def _conv_call(x, mod, gains, win, cw, wout, layer, j):
    b, s, d = x.shape
    tq = TOKEN_TILE
    x_spec, mod_spec = _token_specs(x, mod, layer, tq)
    return pl.pallas_call(
        _conv_kernel,
        grid=(b, s // tq),
        in_specs=[
            x_spec, mod_spec,
            _resident((None, None, 1, d), (layer, 1)),
            _resident((None,) + win.shape[1:], (j,)),
            _resident((None,) + cw.shape[1:], (j,)),
            _resident((None,) + wout.shape[1:], (j,)),
        ],
        out_specs=x_spec,
        out_shape=jax.ShapeDtypeStruct(x.shape, F32),
        scratch_shapes=[pltpu.VMEM((tq + 8, d), F32)],
        compiler_params=_GRID_PARAMS,
        name=f"conv_l{layer}",
    )(x, mod, gains, win, cw, wout)


def kernel(x, c, positions, norm_gain, w_ada, b_ada, w_ffn_up, w_ffn_down, attn_w_qkv,
           attn_b_qkv, attn_q_gain, attn_k_gain, attn_sinks, attn_w_o, attn_b_o,
           conv_w_in, conv_w, conv_w_out):
    depth = w_ada.shape[0]
    b, s, d = x.shape
    n_ada = w_ada.shape[2] // d
    mod = _ada_call(c, w_ada, b_ada).reshape(depth, b, n_ada, d)
    cos_t, sin_t = _rope_call(positions)
    gains = norm_gain.reshape(depth, -1, 1, d)

    wup = w_ffn_up.astype(BF16)
    wdn = w_ffn_down.astype(BF16)

    q_cols, k_cols, o_rows, heads = _attn_layout()
    n_attn = attn_w_qkv.shape[0]
    qkv_cols = np.concatenate([q_cols, d + k_cols, np.arange(d + KV_WIDTH, d + 2 * KV_WIDTH)])
    wqkv = attn_w_qkv[:, :, qkv_cols].astype(BF16)
    bqkv = attn_b_qkv[:, qkv_cols].reshape(n_attn, 1, -1)
    qg = _slice_gain(attn_q_gain)
    kg = _slice_gain(attn_k_gain)
    sinks = attn_sinks[:, heads]
    wo = attn_w_o[:, o_rows, :].astype(BF16)
    bo = attn_b_o.reshape(n_attn, 1, d)

    win = conv_w_in.astype(BF16)
    wout = conv_w_out.astype(BF16)

    for i in range(depth):
        x = _ffn_call(x, mod, gains, wup, wdn, i, 0)
        j = i // 2
        if i % 2 == 0:
            x = _attn_call(x, mod, gains, cos_t, sin_t, wqkv, bqkv, qg, kg, sinks, wo, bo, i, j)
        else:
            x = _conv_call(x, mod, gains, win, conv_w, wout, i, j)
        x = _ffn_call(x, mod, gains, wup, wdn, i, 2)
    return x
```

```python
import functools
import math

import numpy as np
import jax
import jax.numpy as jnp
from jax import lax
from jax.experimental import pallas as pl
from jax.experimental.pallas import tpu as pltpu

F32 = jnp.float32
BF16 = jnp.bfloat16

EPS = 1e-6
ROPE_THETA = 10000.0
N_HEADS = 16
N_KV_HEADS = 4
HEAD_DIM = 64
GROUP = N_HEADS // N_KV_HEADS
BLOCK = 128
CONV_WIDTH = 3
LANES = 128
MXU_COLS = 256
FF_CHUNK = MXU_COLS
TOKEN_TILE = 1024
CONV_ROWS = 512
FFN_TILE = 1024
FFN_ROWS = 512
FFN_HANDOVER_AT = 8
ATTN_TILE = 1024
ROPE_ROWS = 2048
ADA_COLS = 1536
VMEM_LIMIT = 56 * 1024 * 1024
LOG2E = math.log2(math.e)
KV_WIDTH = N_KV_HEADS * HEAD_DIM
Q_SLICES = N_HEADS * HEAD_DIM // LANES
KV_SLICES = KV_WIDTH // LANES
STACK = Q_SLICES // KV_SLICES


def _resident(shape, index):
    full = tuple(index) + (0,) * (len(shape) - len(index))
    return pl.BlockSpec(shape, lambda *_: full, pipeline_mode=pl.Buffered(1))


def _sigmoid(x):
    return 1.0 / (1.0 + jnp.exp(-x))


def _mod_norm(x, gain, shift, scale):
    ms = jnp.mean(x * x, axis=-1, keepdims=True)
    y = (x * lax.rsqrt(ms + EPS)) * gain
    return y * (1.0 + scale) + shift


def _mod_rows(mod_ref, sub):
    shift = mod_ref[3 * sub + 0:3 * sub + 1, :]
    scale = mod_ref[3 * sub + 1:3 * sub + 2, :]
    gate = mod_ref[3 * sub + 2:3 * sub + 3, :]
    return shift, scale, gate


def _token_specs(x, mod, layer, tile, seq):
    _, d = x.shape
    per_seq = seq // tile
    return (pl.BlockSpec((tile, d), lambda t: (t, 0)),
            pl.BlockSpec((None, None) + mod.shape[2:], lambda t: (layer, t // per_seq, 0, 0)))


_GRID_PARAMS = pltpu.CompilerParams(
    dimension_semantics=("arbitrary", "arbitrary"), vmem_limit_bytes=VMEM_LIMIT)
_TOKEN_PARAMS = pltpu.CompilerParams(
    dimension_semantics=("arbitrary",), vmem_limit_bytes=VMEM_LIMIT)


def _ada_kernel(c_ref, w_ref, b_ref, o_ref):
    c = c_ref[...]
    ca = (c * _sigmoid(c)).astype(BF16)
    o_ref[...] = jnp.dot(ca, w_ref[...].astype(BF16), preferred_element_type=F32) + b_ref[...]


def _ada_call(c, w_ada, b_ada):
    depth, d, n = w_ada.shape
    b = c.shape[0]
    tn = ADA_COLS
    return pl.pallas_call(
        _ada_kernel,
        grid=(depth, n // tn),
        in_specs=[
            pl.BlockSpec((b, d), lambda l, j: (0, 0)),
            pl.BlockSpec((None, d, tn), lambda l, j: (l, 0, j)),
            pl.BlockSpec((None, 1, tn), lambda l, j: (l, 0, j)),
        ],
        out_specs=pl.BlockSpec((None, b, tn), lambda l, j: (l, 0, j)),
        out_shape=jax.ShapeDtypeStruct((depth, b, n), F32),
        compiler_params=_GRID_PARAMS,
        name="ada_mod",
    )(c, w_ada, b_ada.reshape(depth, 1, n))


def _rope_kernel(pos_ref, invf_ref, sign_ref, cos_ref, sin_ref):
    ang = pos_ref[...].astype(F32) * invf_ref[...]
    per_row = LANES // (HEAD_DIM // 2)
    rows = ang.shape[0]
    group = lax.broadcasted_iota(jnp.int32, ang.shape, 1) // (HEAD_DIM // 2)
    for table, out_ref, sign in ((jnp.cos(ang), cos_ref, None), (jnp.sin(ang), sin_ref, sign_ref)):
        for m in range(per_row):
            one = jnp.where(group == m, table, 0.0)
            one = one + pltpu.roll(one, HEAD_DIM // 2, 1)
            one = one + pltpu.roll(one, HEAD_DIM, 1)
            if sign is not None:
                one = one * sign[...]
            out_ref[pl.ds(m, rows, stride=per_row), :] = one


def _rope_call(positions):
    b, s = positions.shape
    half = HEAD_DIM // 2
    per_row = LANES // half
    rows = b * s // per_row
    inv_freq = ROPE_THETA ** (-jnp.arange(0, HEAD_DIM, 2, dtype=F32) / HEAD_DIM)
    invf = jnp.tile(inv_freq, per_row).reshape(1, LANES)
    sign = jnp.where(jnp.arange(LANES) < LANES // 2, -1.0, 1.0).astype(F32).reshape(1, LANES)
    pos_rep = jnp.repeat(positions.reshape(rows, per_row), half, axis=1)
    tr = min(ROPE_ROWS, rows)
    const = pl.BlockSpec((1, LANES), lambda i: (0, 0))
    out_spec = pl.BlockSpec((tr * per_row, LANES), lambda i: (i, 0))
    return pl.pallas_call(
        _rope_kernel,
        grid=(rows // tr,),
        in_specs=[pl.BlockSpec((tr, LANES), lambda i: (i, 0)), const, const],
        out_specs=[out_spec, out_spec],
        out_shape=[jax.ShapeDtypeStruct((b * s, LANES), F32)] * 2,
        compiler_params=pltpu.CompilerParams(dimension_semantics=("arbitrary",)),
        name="rope_tables",
    )(pos_rep, invf, sign)


def _ffn_kernel(x_ref, xn_ref, mod_ref, modn_ref, gain_ref, wup_ref, wdn_ref, o_ref,
                h_scr, a_scr, *, sub):
    shift, scale, gate = _mod_rows(mod_ref, sub)
    shift_n, scale_n, _ = _mod_rows(modn_ref, sub)
    gain = gain_ref[...]
    d_ff = wdn_ref.shape[0]
    n_groups = h_scr.shape[0]
    n_chunks = d_ff // FF_CHUNK
    rows_of = lambda g: slice(g * FFN_ROWS, (g + 1) * FFN_ROWS)

    def prepare(g):
        if g < n_groups:
            h = _mod_norm(x_ref[rows_of(g), :], gain, shift, scale)
        else:
            h = _mod_norm(xn_ref[...], gain, shift_n, scale_n)
        h_scr[g % n_groups] = h.astype(BF16)

    def up_chunk(g, c):
        lo = c * FF_CHUNK
        h = h_scr[g]
        gg = jnp.dot(h, wup_ref[:, lo:lo + FF_CHUNK], preferred_element_type=F32)
        uu = jnp.dot(h, wup_ref[:, d_ff + lo:d_ff + lo + FF_CHUNK], preferred_element_type=F32)
        a_scr[rows_of(g), lo:lo + FF_CHUNK] = ((gg * _sigmoid(gg)) * uu).astype(BF16)

    @pl.when(pl.program_id(0) == 0)
    def _():
        prepare(0)
        up_chunk(0, 0)

    for g in range(n_groups):
        rows = rows_of(g)
        for c in range(1, n_chunks):
            up_chunk(g, c)
            if c == FFN_HANDOVER_AT:
                prepare(g + 1)
                up_chunk((g + 1) % n_groups, 0)
        y = jnp.dot(a_scr[rows, :], wdn_ref[...], preferred_element_type=F32)
        o_ref[rows, :] = x_ref[rows, :] + (0.5 * gate) * y


def _ffn_call(x, mod, gains, wup, wdn, layer, sub, seq):
    n_rows, d = x.shape
    ts = FFN_TILE
    d_ff = wdn.shape[2]
    which = sub // 2
    n_tiles = n_rows // ts
    groups = ts // FFN_ROWS
    per_seq = seq // ts
    x_spec, mod_spec = _token_specs(x, mod, layer, ts, seq)
    nxt = lambda t: jnp.minimum(t + 1, n_tiles - 1)
    xn_spec = pl.BlockSpec((FFN_ROWS, d), lambda t: (nxt(t) * groups, 0))
    modn_spec = pl.BlockSpec((None, None) + mod.shape[2:],
                             lambda t: (layer, nxt(t) // per_seq, 0, 0))
    return pl.pallas_call(
        functools.partial(_ffn_kernel, sub=sub),
        grid=(n_tiles,),
        in_specs=[
            x_spec, xn_spec, mod_spec, modn_spec,
            _resident((None, None, 1, d), (layer, sub)),
            _resident((None, None, d, 2 * d_ff), (layer, which)),
            _resident((None, None, d_ff, d), (layer, which)),
        ],
        out_specs=x_spec,
        out_shape=jax.ShapeDtypeStruct(x.shape, F32),
        scratch_shapes=[pltpu.VMEM((groups, FFN_ROWS, d), BF16), pltpu.VMEM((ts, d_ff), BF16)],
        compiler_params=_TOKEN_PARAMS,
        name=f"ffn_l{layer}_s{sub}",
    )(x, x, mod, mod, gains, wup, wdn)


def _norm_rope(t, ms, gain, cos, sin):
    tn = (t * lax.rsqrt(ms + EPS)) * gain
    outs = []
    for j in range(t.shape[1] // LANES):
        v = tn[:, j * LANES:(j + 1) * LANES]
        outs.append(v * cos + pltpu.roll(v, LANES // 2, 1) * sin)
    return outs


def _attn_kernel(x_ref, mod_ref, gain_ref, cos_ref, sin_ref, wqkv_ref,
                 bqkv_ref, qg_ref, kg_ref, sink_ref, wo_ref, bo_ref, o_ref,
                 q_scr, k_scr, v_scr, att_scr, sink_scr, sc_scr, *, attn_layer, per_seq):
    tq, d = x_ref.shape
    nb = tq // BLOCK
    rows_stack = STACK * BLOCK
    first = pl.program_id(0) % per_seq == 0

    @pl.when(first)
    def _():
        k_scr[0:BLOCK, :] = jnp.zeros((BLOCK, KV_WIDTH), BF16)
        v_scr[0:BLOCK, :] = jnp.zeros((BLOCK, KV_WIDTH), BF16)

    @pl.when(pl.program_id(0) == 0)
    def _():
        for sg in range(KV_SLICES):
            for half in range(2):
                for g in range(STACK):
                    val = sink_ref[attn_layer, 2 * (sg * STACK + g) + half] * LOG2E
                    sink_scr[2 * sg + half, g * BLOCK:(g + 1) * BLOCK, :] = jnp.full(
                        (BLOCK, LANES), val, F32)

    shift, scale, gate = _mod_rows(mod_ref, 1)
    h = _mod_norm(x_ref[...], gain_ref[...], shift, scale).astype(BF16)

    cos = cos_ref[...]
    sin = sin_ref[...]

    def head_of(col):
        return (col // LANES) * 2 + (col // (HEAD_DIM // 2)) % 2

    r = head_of(lax.broadcasted_iota(jnp.int32, (MXU_COLS, MXU_COLS), 0))
    c = head_of(lax.broadcasted_iota(jnp.int32, (MXU_COLS, MXU_COLS), 1))
    group_mean = jnp.where(r == c, 1.0 / HEAD_DIM, 0.0).astype(BF16)

    n_chunks = (d + 2 * KV_WIDTH) // MXU_COLS
    n_norm = (d + KV_WIDTH) // MXU_COLS
    qgain = qg_ref[...] * (HEAD_DIM ** -0.5 * LOG2E)

    def project(cc):
        cols = slice(cc * MXU_COLS, (cc + 1) * MXU_COLS)
        return jnp.dot(h, wqkv_ref[:, cols], preferred_element_type=F32) + bqkv_ref[:, cols]

    def finish(cc, t):
        if cc == n_norm:
            v_scr[BLOCK:, :] = t.astype(BF16)
            return
        ms = jnp.dot((t * t).astype(BF16), group_mean, preferred_element_type=F32)
        is_q = cc < d // MXU_COLS
        outs = _norm_rope(t, ms, qgain if is_q else kg_ref[...], cos, sin)
        for j, o in enumerate(outs):
            ob = o.astype(BF16)
            if is_q:
                sg, g = divmod(cc * (MXU_COLS // LANES) + j, STACK)
                for n in range(nb):
                    q_scr[sg, n, g * BLOCK:(g + 1) * BLOCK, :] = ob[n * BLOCK:(n + 1) * BLOCK]
            else:
                k_scr[BLOCK:, j * LANES:(j + 1) * LANES] = ob

    ahead = project(0)
    for cc in range(n_chunks):
        t = ahead
        if cc + 1 < n_chunks:
            ahead = project(cc + 1)
        finish(cc, t)

    qi = lax.broadcasted_iota(jnp.int32, (rows_stack, LANES), 0) % BLOCK
    kc = lax.broadcasted_iota(jnp.int32, (rows_stack, LANES), 1)
    take_prev = kc > qi
    low_q = kc < HEAD_DIM
    kv_lane = lax.broadcasted_iota(jnp.int32, (2 * BLOCK, LANES), 1)
    k_first = (kv_lane // (HEAD_DIM // 2)) % 2 == 0
    v_first = kv_lane < HEAD_DIM
    no_prev = jnp.where(first, -jnp.inf, 0.0)
    zero_k = jnp.zeros((2 * BLOCK, LANES), BF16)
    one_k = jnp.ones((2 * BLOCK, LANES), BF16)
    zero_p = jnp.zeros((rows_stack, LANES), F32)

    def block_rows(n, count):
        start = n * BLOCK if isinstance(n, int) else pl.multiple_of(n * BLOCK, BLOCK)
        return pl.ds(start, count * BLOCK)

    def scores(n):
        slot = n % sc_scr.shape[0]
        for sg in range(KV_SLICES):
            kb = k_scr[block_rows(n, 2), sg * LANES:(sg + 1) * LANES]
            q = q_scr[sg, n]
            for half in range(2):
                kk = jnp.where(k_first, kb, zero_k) if half == 0 else jnp.where(k_first, zero_k, kb)
                sc_scr[slot, 2 * sg + half] = lax.dot_general(
                    q, kk, (((1,), (1,)), ((), ())), preferred_element_type=F32)

    def attend(n):
        slot = n % sc_scr.shape[0]
        bias = jnp.where(n == 0, no_prev, 0.0)
        for sg in range(KV_SLICES):
            vb = v_scr[block_rows(n, 2), sg * LANES:(sg + 1) * LANES]
            res = []
            for half in range(2):
                s = jnp.where(take_prev, sc_scr[slot, 2 * sg + half, :, :BLOCK] + bias,
                              sc_scr[slot, 2 * sg + half, :, BLOCK:])
                sink = sink_scr[2 * sg + half]
                m = jnp.maximum(jnp.max(s, axis=-1, keepdims=True), sink)
                p = jnp.exp2(s - m)
                p_cat = jnp.concatenate([jnp.where(take_prev, p, zero_p).astype(BF16),
                                         jnp.where(take_prev, zero_p, p).astype(BF16)], axis=1)
                vv = jnp.where(v_first, vb, one_k) if half == 0 else jnp.where(v_first, one_k, vb)
                o = jnp.dot(p_cat, vv, preferred_element_type=F32)
                res.append((o, o + jnp.exp2(sink - m)))
            (o_lo, d_lo), (o_hi, d_hi) = res
            numer = jnp.where(low_q, o_lo, o_hi)
            den = pltpu.roll(jnp.where(low_q, d_hi, d_lo), HEAD_DIM, 1)
            out = (numer * (1.0 / den)).astype(BF16)
            for g in range(STACK):
                sl = sg * STACK + g
                att_scr[block_rows(n, 1), sl * LANES:(sl + 1) * LANES] = (
                    out[g * BLOCK:(g + 1) * BLOCK])

    scores(0)

    def trip(n, carry):
        attend(n)
        scores(n + 1)
        return carry

    lax.fori_loop(0, nb - 1, trip, 0)
    attend(nb - 1)

    k_scr[0:BLOCK, :] = k_scr[tq:tq + BLOCK, :]
    v_scr[0:BLOCK, :] = v_scr[tq:tq + BLOCK, :]

    y = jnp.dot(att_scr[...], wo_ref[...], preferred_element_type=F32) + bo_ref[...]
    o_ref[...] = x_ref[...] + gate * y


def _slice_cols(head_a, head_b):
    half = HEAD_DIM // 2
    cols = []
    for part in range(2):
        for head in (head_a, head_b):
            start = head * HEAD_DIM + part * half
            cols.extend(range(start, start + half))
    return cols


def _attn_layout():
    q_cols, k_cols, heads = [], [], []
    for sl in range(Q_SLICES):
        pair, g = divmod(sl, STACK)
        a, b = (2 * pair) * GROUP + g, (2 * pair + 1) * GROUP + g
        heads.extend([a, b])
        q_cols.extend(_slice_cols(a, b))
    for sg in range(KV_SLICES):
        k_cols.extend(_slice_cols(2 * sg, 2 * sg + 1))
    as_idx = lambda v: np.asarray(v, np.int32)
    return as_idx(q_cols), as_idx(k_cols), as_idx(heads)


def _slice_gain(gain):
    half = HEAD_DIM // 2
    lo, hi = gain[:, :half], gain[:, half:]
    one_slice = jnp.concatenate([lo, lo, hi, hi], axis=1)
    reps = MXU_COLS // LANES
    return jnp.tile(one_slice, (1, reps)).reshape(gain.shape[0], 1, MXU_COLS)


def _attn_call(x, mod, gains, cos_t, sin_t, wqkv, bqkv, qg, kg, sinks, wo, bo, layer, j, seq):
    n_rows, d = x.shape
    tq = ATTN_TILE
    nb = tq // BLOCK
    x_spec, mod_spec = _token_specs(x, mod, layer, tq, seq)
    rope_spec = pl.BlockSpec((tq, LANES), lambda t: (t, 0))
    return pl.pallas_call(
        functools.partial(_attn_kernel, attn_layer=j, per_seq=seq // tq),
        grid=(n_rows // tq,),
        in_specs=[
            x_spec, mod_spec,
            _resident((None, None, 1, d), (layer, 1)),
            rope_spec, rope_spec,
            _resident((None,) + wqkv.shape[1:], (j,)),
            _resident((None,) + bqkv.shape[1:], (j,)),
            _resident((None,) + qg.shape[1:], (j,)),
            _resident((None,) + kg.shape[1:], (j,)),
            pl.BlockSpec(memory_space=pltpu.SMEM),
            _resident((None,) + wo.shape[1:], (j,)),
            _resident((None,) + bo.shape[1:], (j,)),
        ],
        out_specs=x_spec,
        out_shape=jax.ShapeDtypeStruct(x.shape, F32),
        scratch_shapes=[
            pltpu.VMEM((KV_SLICES, nb, STACK * BLOCK, LANES), BF16),
            pltpu.VMEM((BLOCK + tq, KV_WIDTH), BF16),
            pltpu.VMEM((BLOCK + tq, KV_WIDTH), BF16),
            pltpu.VMEM((tq, d), BF16),
            pltpu.VMEM((2 * KV_SLICES, STACK * BLOCK, LANES), F32),
            pltpu.VMEM((2, 2 * KV_SLICES, STACK * BLOCK, 2 * BLOCK), F32),
        ],
        compiler_params=_TOKEN_PARAMS,
        name=f"attn_l{layer}",
    )(x, mod, gains, cos_t, sin_t, wqkv, bqkv, qg, kg, sinks, wo, bo)


def _conv_kernel(x_ref, mod_ref, gain_ref, win_ref, cw_ref, wout_ref, o_ref, u_scr, *, per_seq):
    tq, d = x_ref.shape
    pad = u_scr.shape[0] - tq

    @pl.when(pl.program_id(0) % per_seq == 0)
    def _():
        u_scr[0:pad, :] = jnp.zeros((pad, d), F32)

    shift, scale, gate = _mod_rows(mod_ref, 1)
    gain = gain_ref[...]
    for r0 in range(0, tq, CONV_ROWS):
        rows = slice(r0, r0 + CONV_ROWS)
        x = x_ref[rows, :]
        h = _mod_norm(x, gain, shift, scale).astype(BF16)
        gc = jnp.dot(h, win_ref[:, d:2 * d], preferred_element_type=F32)
        v = jnp.dot(h, win_ref[:, 2 * d:3 * d], preferred_element_type=F32)
        gb = jnp.dot(h, win_ref[:, 0:d], preferred_element_type=F32)
        u_scr[pad + r0:pad + r0 + CONV_ROWS, :] = gc * v
        conv = cw_ref[CONV_WIDTH - 1:CONV_WIDTH, :] * u_scr[pad + r0:pad + r0 + CONV_ROWS, :]
        for k in range(1, CONV_WIDTH):
            w = cw_ref[CONV_WIDTH - 1 - k:CONV_WIDTH - k, :]
            conv = conv + w * u_scr[pad + r0 - k:pad + r0 - k + CONV_ROWS, :]
        y = jnp.dot((gb * conv).astype(BF16), wout_ref[...], preferred_element_type=F32)
        o_ref[rows, :] = x + gate * y
    u_scr[0:pad, :] = u_scr[tq:tq + pad, :]


def _conv_call(x, mod, gains, win, cw, wout, layer, j, seq):
    n_rows, d = x.shape
    tq = TOKEN_TILE
    x_spec, mod_spec = _token_specs(x, mod, layer, tq, seq)
    return pl.pallas_call(
        functools.partial(_conv_kernel, per_seq=seq // tq),
        grid=(n_rows // tq,),
        in_specs=[
            x_spec, mod_spec,
            _resident((None, None, 1, d), (layer, 1)),
            _resident((None,) + win.shape[1:], (j,)),
            _resident((None,) + cw.shape[1:], (j,)),
            _resident((None,) + wout.shape[1:], (j,)),
        ],
        out_specs=x_spec,
        out_shape=jax.ShapeDtypeStruct(x.shape, F32),
        scratch_shapes=[pltpu.VMEM((tq + 8, d), F32)],
        compiler_params=_TOKEN_PARAMS,
        name=f"conv_l{layer}",
    )(x, mod, gains, win, cw, wout)


def kernel(x, c, positions, norm_gain, w_ada, b_ada, w_ffn_up, w_ffn_down, attn_w_qkv,
           attn_b_qkv, attn_q_gain, attn_k_gain, attn_sinks, attn_w_o, attn_b_o,
           conv_w_in, conv_w, conv_w_out):
    depth = w_ada.shape[0]
    b, s, d = x.shape
    n_ada = w_ada.shape[2] // d
    mod = _ada_call(c, w_ada, b_ada).reshape(depth, b, n_ada, d)
    cos_t, sin_t = _rope_call(positions)
    gains = norm_gain.reshape(depth, -1, 1, d)

    wup = w_ffn_up.astype(BF16)
    wdn = w_ffn_down.astype(BF16)

    q_cols, k_cols, heads = _attn_layout()
    n_attn = attn_w_qkv.shape[0]
    qkv_cols = np.concatenate([q_cols, d + k_cols, np.arange(d + KV_WIDTH, d + 2 * KV_WIDTH)])
    chunk = HEAD_DIM // 2
    src_chunks = qkv_cols.reshape(-1, chunk)[:, 0] // chunk
    wqkv = attn_w_qkv.reshape(n_attn, d, -1, chunk)[:, :, src_chunks, :]
    wqkv = wqkv.reshape(n_attn, d, -1).astype(BF16)
    bqkv = attn_b_qkv[:, qkv_cols].reshape(n_attn, 1, -1)
    qg = _slice_gain(attn_q_gain)
    kg = _slice_gain(attn_k_gain)
    sinks = attn_sinks[:, heads]
    wo = attn_w_o.reshape(n_attn, N_HEADS, HEAD_DIM, d)[:, heads].reshape(n_attn, d, d).astype(BF16)
    bo = attn_b_o.reshape(n_attn, 1, d)

    win = conv_w_in.astype(BF16)
    wout = conv_w_out.astype(BF16)

    x = x.reshape(b * s, d)
    for i in range(depth):
        x = _ffn_call(x, mod, gains, wup, wdn, i, 0, s)
        j = i // 2
        if i % 2 == 0:
            x = _attn_call(x, mod, gains, cos_t, sin_t, wqkv, bqkv, qg, kg, sinks, wo, bo,
                           i, j, s)
        else:
            x = _conv_call(x, mod, gains, win, conv_w, wout, i, j, s)
        x = _ffn_call(x, mod, gains, wup, wdn, i, 2, s)
    return x.reshape(b, s, d)
```

```python
import functools
import math

import numpy as np
import jax
import jax.numpy as jnp
from jax import lax
from jax.experimental import pallas as pl
from jax.experimental.pallas import tpu as pltpu

F32 = jnp.float32
BF16 = jnp.bfloat16

EPS = 1e-6
ROPE_THETA = 10000.0
N_HEADS = 16
N_KV_HEADS = 4
HEAD_DIM = 64
GROUP = N_HEADS // N_KV_HEADS
BLOCK = 128
CONV_WIDTH = 3
LANES = 128
MXU_COLS = 256
FF_CHUNK = MXU_COLS
TOKEN_TILE = 1024
CONV_ROWS = 512
FFN_TILE = 1024
FFN_ROWS = 512
FFN_HANDOVER_AT = 8
CAST_DOWN_ROWS = 128
ATTN_TILE = 1024
ROPE_ROWS = 2048
ADA_COLS = 1536
VMEM_LIMIT = 56 * 1024 * 1024
LOG2E = math.log2(math.e)
KV_WIDTH = N_KV_HEADS * HEAD_DIM
Q_SLICES = N_HEADS * HEAD_DIM // LANES
KV_SLICES = KV_WIDTH // LANES
STACK = Q_SLICES // KV_SLICES


def _resident(shape, index):
    full = tuple(index) + (0,) * (len(shape) - len(index))
    return pl.BlockSpec(shape, lambda *_: full, pipeline_mode=pl.Buffered(1))


def _sigmoid(x):
    return 1.0 / (1.0 + jnp.exp(-x))


def _mod_norm(x, gain, shift, scale):
    ms = jnp.mean(x * x, axis=-1, keepdims=True)
    y = (x * lax.rsqrt(ms + EPS)) * gain
    return y * (1.0 + scale) + shift


def _mod_rows(mod_ref, sub):
    shift = mod_ref[3 * sub + 0:3 * sub + 1, :]
    scale = mod_ref[3 * sub + 1:3 * sub + 2, :]
    gate = mod_ref[3 * sub + 2:3 * sub + 3, :]
    return shift, scale, gate


def _token_specs(x, mod, layer, tile, seq):
    _, d = x.shape
    per_seq = seq // tile
    return (pl.BlockSpec((tile, d), lambda t: (t, 0)),
            pl.BlockSpec((None, None) + mod.shape[2:], lambda t: (layer, t // per_seq, 0, 0)))


_GRID_PARAMS = pltpu.CompilerParams(
    dimension_semantics=("arbitrary", "arbitrary"), vmem_limit_bytes=VMEM_LIMIT)
_TOKEN_PARAMS = pltpu.CompilerParams(
    dimension_semantics=("arbitrary",), vmem_limit_bytes=VMEM_LIMIT)


def _ada_kernel(c_ref, w_ref, b_ref, o_ref):
    c = c_ref[...]
    ca = (c * _sigmoid(c)).astype(BF16)
    o_ref[...] = jnp.dot(ca, w_ref[...].astype(BF16), preferred_element_type=F32) + b_ref[...]


def _ada_call(c, w_ada, b_ada):
    depth, d, n = w_ada.shape
    b = c.shape[0]
    tn = ADA_COLS
    return pl.pallas_call(
        _ada_kernel,
        grid=(depth, n // tn),
        in_specs=[
            pl.BlockSpec((b, d), lambda l, j: (0, 0)),
            pl.BlockSpec((None, d, tn), lambda l, j: (l, 0, j)),
            pl.BlockSpec((None, 1, tn), lambda l, j: (l, 0, j)),
        ],
        out_specs=pl.BlockSpec((None, b, tn), lambda l, j: (l, 0, j)),
        out_shape=jax.ShapeDtypeStruct((depth, b, n), F32),
        compiler_params=_GRID_PARAMS,
        name="ada_mod",
    )(c, w_ada, b_ada.reshape(depth, 1, n))


def _rope_kernel(pos_ref, invf_ref, sign_ref, cos_ref, sin_ref):
    ang = pos_ref[...].astype(F32) * invf_ref[...]
    per_row = LANES // (HEAD_DIM // 2)
    rows = ang.shape[0]
    group = lax.broadcasted_iota(jnp.int32, ang.shape, 1) // (HEAD_DIM // 2)
    for table, out_ref, sign in ((jnp.cos(ang), cos_ref, None), (jnp.sin(ang), sin_ref, sign_ref)):
        for m in range(per_row):
            one = jnp.where(group == m, table, 0.0)
            one = one + pltpu.roll(one, HEAD_DIM // 2, 1)
            one = one + pltpu.roll(one, HEAD_DIM, 1)
            if sign is not None:
                one = one * sign[...]
            out_ref[pl.ds(m, rows, stride=per_row), :] = one


def _rope_call(positions):
    b, s = positions.shape
    half = HEAD_DIM // 2
    per_row = LANES // half
    rows = b * s // per_row
    inv_freq = ROPE_THETA ** (-jnp.arange(0, HEAD_DIM, 2, dtype=F32) / HEAD_DIM)
    invf = jnp.tile(inv_freq, per_row).reshape(1, LANES)
    sign = jnp.where(jnp.arange(LANES) < LANES // 2, -1.0, 1.0).astype(F32).reshape(1, LANES)
    pos_rep = jnp.repeat(positions.reshape(rows, per_row), half, axis=1)
    tr = min(ROPE_ROWS, rows)
    const = pl.BlockSpec((1, LANES), lambda i: (0, 0))
    out_spec = pl.BlockSpec((tr * per_row, LANES), lambda i: (i, 0))
    return pl.pallas_call(
        _rope_kernel,
        grid=(rows // tr,),
        in_specs=[pl.BlockSpec((tr, LANES), lambda i: (i, 0)), const, const],
        out_specs=[out_spec, out_spec],
        out_shape=[jax.ShapeDtypeStruct((b * s, LANES), F32)] * 2,
        compiler_params=pltpu.CompilerParams(dimension_semantics=("arbitrary",)),
        name="rope_tables",
    )(pos_rep, invf, sign)


def _ffn_kernel(x_ref, xn_ref, mod_ref, modn_ref, gain_ref, wup_ref, wdn_ref, *rest,
                sub, cast_next):
    if cast_next:
        wup_f32, wdn_f32, o_ref, wup_next, wdn_next, h_scr, a_scr = rest
        wup_next[...] = wup_f32[...].astype(BF16)
        wdn_next[...] = wdn_f32[...].astype(BF16)
    else:
        o_ref, h_scr, a_scr = rest
    shift, scale, gate = _mod_rows(mod_ref, sub)
    shift_n, scale_n, _ = _mod_rows(modn_ref, sub)
    gain = gain_ref[...]
    d_ff = wdn_ref.shape[0]
    n_groups = h_scr.shape[0]
    n_chunks = d_ff // FF_CHUNK
    rows_of = lambda g: slice(g * FFN_ROWS, (g + 1) * FFN_ROWS)

    def prepare(g):
        if g < n_groups:
            h = _mod_norm(x_ref[rows_of(g), :], gain, shift, scale)
        else:
            h = _mod_norm(xn_ref[...], gain, shift_n, scale_n)
        h_scr[g % n_groups] = h.astype(BF16)

    def up_chunk(g, c):
        lo = c * FF_CHUNK
        h = h_scr[g]
        gg = jnp.dot(h, wup_ref[:, lo:lo + FF_CHUNK], preferred_element_type=F32)
        uu = jnp.dot(h, wup_ref[:, d_ff + lo:d_ff + lo + FF_CHUNK], preferred_element_type=F32)
        a_scr[rows_of(g), lo:lo + FF_CHUNK] = ((gg * _sigmoid(gg)) * uu).astype(BF16)

    @pl.when(pl.program_id(0) == 0)
    def _():
        prepare(0)
        up_chunk(0, 0)

    for g in range(n_groups):
        rows = rows_of(g)
        for c in range(1, n_chunks):
            up_chunk(g, c)
            if c == FFN_HANDOVER_AT:
                prepare(g + 1)
                up_chunk((g + 1) % n_groups, 0)
        y = jnp.dot(a_scr[rows, :], wdn_ref[...], preferred_element_type=F32)
        o_ref[rows, :] = x_ref[rows, :] + (0.5 * gate) * y


def _ffn_call(x, mod, gains, wup, wdn, layer, sub, seq, next_f32=None):
    n_rows, d = x.shape
    ts = FFN_TILE
    d_ff = wdn.shape[0]
    n_tiles = n_rows // ts
    groups = ts // FFN_ROWS
    per_seq = seq // ts
    x_spec, mod_spec = _token_specs(x, mod, layer, ts, seq)
    nxt = lambda t: jnp.minimum(t + 1, n_tiles - 1)
    xn_spec = pl.BlockSpec((FFN_ROWS, d), lambda t: (nxt(t) * groups, 0))
    modn_spec = pl.BlockSpec((None, None) + mod.shape[2:],
                             lambda t: (layer, nxt(t) // per_seq, 0, 0))
    in_specs = [x_spec, xn_spec, mod_spec, modn_spec,
                _resident((None, None, 1, d), (layer, sub)),
                _resident(wup.shape, ()), _resident(wdn.shape, ())]
    operands = [x, x, mod, mod, gains, wup, wdn]
    out_specs = [x_spec]
    out_shape = [jax.ShapeDtypeStruct(x.shape, F32)]
    if next_f32 is not None:
        w_up_all, w_down_all, (nl, nw) = next_f32
        up_rows = d // n_tiles
        dn_rows = CAST_DOWN_ROWS
        dn_last = d_ff // dn_rows - 1
        in_specs += [
            pl.BlockSpec((None, None, up_rows, 2 * d_ff), lambda t: (nl, nw, t, 0)),
            pl.BlockSpec((None, None, dn_rows, d), lambda t: (nl, nw, jnp.minimum(t, dn_last), 0)),
        ]
        operands += [w_up_all, w_down_all]
        out_specs += [
            pl.BlockSpec((up_rows, 2 * d_ff), lambda t: (t, 0)),
            pl.BlockSpec((dn_rows, d), lambda t: (jnp.minimum(t, dn_last), 0)),
        ]
        out_shape += [jax.ShapeDtypeStruct((d, 2 * d_ff), BF16),
                      jax.ShapeDtypeStruct((d_ff, d), BF16)]
    return pl.pallas_call(
        functools.partial(_ffn_kernel, sub=sub, cast_next=next_f32 is not None),
        grid=(n_tiles,),
        in_specs=in_specs,
        out_specs=out_specs,
        out_shape=out_shape,
        scratch_shapes=[pltpu.VMEM((groups, FFN_ROWS, d), BF16), pltpu.VMEM((ts, d_ff), BF16)],
        compiler_params=_TOKEN_PARAMS,
        name=f"ffn_l{layer}_s{sub}",
    )(*operands)


def _norm_rope(t, ms, gain, cos, sin):
    tn = (t * lax.rsqrt(ms + EPS)) * gain
    outs = []
    for j in range(t.shape[1] // LANES):
        v = tn[:, j * LANES:(j + 1) * LANES]
        outs.append(v * cos + pltpu.roll(v, LANES // 2, 1) * sin)
    return outs


def _attn_kernel(x_ref, mod_ref, gain_ref, cos_ref, sin_ref, wqkv_ref,
                 bqkv_ref, qg_ref, kg_ref, sink_ref, wo_ref, bo_ref, o_ref,
                 q_scr, k_scr, v_scr, att_scr, sink_scr, sc_scr, *, attn_layer, per_seq):
    tq, d = x_ref.shape
    nb = tq // BLOCK
    rows_stack = STACK * BLOCK
    first = pl.program_id(0) % per_seq == 0

    @pl.when(first)
    def _():
        k_scr[0:BLOCK, :] = jnp.zeros((BLOCK, KV_WIDTH), BF16)
        v_scr[0:BLOCK, :] = jnp.zeros((BLOCK, KV_WIDTH), BF16)

    @pl.when(pl.program_id(0) == 0)
    def _():
        for sg in range(KV_SLICES):
            for half in range(2):
                for g in range(STACK):
                    val = sink_ref[attn_layer, 2 * (sg * STACK + g) + half] * LOG2E
                    sink_scr[2 * sg + half, g * BLOCK:(g + 1) * BLOCK, :] = jnp.full(
                        (BLOCK, LANES), val, F32)

    shift, scale, gate = _mod_rows(mod_ref, 1)
    h = _mod_norm(x_ref[...], gain_ref[...], shift, scale).astype(BF16)

    cos = cos_ref[...]
    sin = sin_ref[...]

    def head_of(col):
        return (col // LANES) * 2 + (col // (HEAD_DIM // 2)) % 2

    r = head_of(lax.broadcasted_iota(jnp.int32, (MXU_COLS, MXU_COLS), 0))
    c = head_of(lax.broadcasted_iota(jnp.int32, (MXU_COLS, MXU_COLS), 1))
    group_mean = jnp.where(r == c, 1.0 / HEAD_DIM, 0.0).astype(BF16)

    n_chunks = (d + 2 * KV_WIDTH) // MXU_COLS
    n_norm = (d + KV_WIDTH) // MXU_COLS
    qgain = qg_ref[...] * (HEAD_DIM ** -0.5 * LOG2E)

    def project(cc):
        cols = slice(cc * MXU_COLS, (cc + 1) * MXU_COLS)
        return jnp.dot(h, wqkv_ref[:, cols], preferred_element_type=F32) + bqkv_ref[:, cols]

    def finish(cc, t):
        if cc == n_norm:
            v_scr[BLOCK:, :] = t.astype(BF16)
            return
        ms = jnp.dot((t * t).astype(BF16), group_mean, preferred_element_type=F32)
        is_q = cc < d // MXU_COLS
        outs = _norm_rope(t, ms, qgain if is_q else kg_ref[...], cos, sin)
        for j, o in enumerate(outs):
            ob = o.astype(BF16)
            if is_q:
                sg, g = divmod(cc * (MXU_COLS // LANES) + j, STACK)
                for n in range(nb):
                    q_scr[sg, n, g * BLOCK:(g + 1) * BLOCK, :] = ob[n * BLOCK:(n + 1) * BLOCK]
            else:
                k_scr[BLOCK:, j * LANES:(j + 1) * LANES] = ob

    ahead = project(0)
    for cc in range(n_chunks):
        t = ahead
        if cc + 1 < n_chunks:
            ahead = project(cc + 1)
        finish(cc, t)

    qi = lax.broadcasted_iota(jnp.int32, (rows_stack, LANES), 0) % BLOCK
    kc = lax.broadcasted_iota(jnp.int32, (rows_stack, LANES), 1)
    take_prev = kc > qi
    low_q = kc < HEAD_DIM
    kv_lane = lax.broadcasted_iota(jnp.int32, (2 * BLOCK, LANES), 1)
    k_first = (kv_lane // (HEAD_DIM // 2)) % 2 == 0
    v_first = kv_lane < HEAD_DIM
    no_prev = jnp.where(first, -jnp.inf, 0.0)
    zero_k = jnp.zeros((2 * BLOCK, LANES), BF16)
    one_k = jnp.ones((2 * BLOCK, LANES), BF16)
    zero_p = jnp.zeros((rows_stack, LANES), F32)

    def block_rows(n, count):
        start = n * BLOCK if isinstance(n, int) else pl.multiple_of(n * BLOCK, BLOCK)
        return pl.ds(start, count * BLOCK)

    def scores(n):
        slot = n % sc_scr.shape[0]
        for sg in range(KV_SLICES):
            kb = k_scr[block_rows(n, 2), sg * LANES:(sg + 1) * LANES]
            q = q_scr[sg, n]
            for half in range(2):
                kk = jnp.where(k_first, kb, zero_k) if half == 0 else jnp.where(k_first, zero_k, kb)
                sc_scr[slot, 2 * sg + half] = lax.dot_general(
                    q, kk, (((1,), (1,)), ((), ())), preferred_element_type=F32)

    def attend(n):
        slot = n % sc_scr.shape[0]
        bias = jnp.where(n == 0, no_prev, 0.0)
        for sg in range(KV_SLICES):
            vb = v_scr[block_rows(n, 2), sg * LANES:(sg + 1) * LANES]
            res = []
            for half in range(2):
                s = jnp.where(take_prev, sc_scr[slot, 2 * sg + half, :, :BLOCK] + bias,
                              sc_scr[slot, 2 * sg + half, :, BLOCK:])
                sink = sink_scr[2 * sg + half]
                m = jnp.maximum(jnp.max(s, axis=-1, keepdims=True), sink)
                p = jnp.exp2(s - m)
                p_cat = jnp.concatenate([jnp.where(take_prev, p, zero_p).astype(BF16),
                                         jnp.where(take_prev, zero_p, p).astype(BF16)], axis=1)
                vv = jnp.where(v_first, vb, one_k) if half == 0 else jnp.where(v_first, one_k, vb)
                o = jnp.dot(p_cat, vv, preferred_element_type=F32)
                res.append((o, o + jnp.exp2(sink - m)))
            (o_lo, d_lo), (o_hi, d_hi) = res
            numer = jnp.where(low_q, o_lo, o_hi)
            den = pltpu.roll(jnp.where(low_q, d_hi, d_lo), HEAD_DIM, 1)
            out = (numer * (1.0 / den)).astype(BF16)
            for g in range(STACK):
                sl = sg * STACK + g
                att_scr[block_rows(n, 1), sl * LANES:(sl + 1) * LANES] = (
                    out[g * BLOCK:(g + 1) * BLOCK])

    scores(0)

    def trip(n, carry):
        attend(n)
        scores(n + 1)
        return carry

    lax.fori_loop(0, nb - 1, trip, 0)
    attend(nb - 1)

    k_scr[0:BLOCK, :] = k_scr[tq:tq + BLOCK, :]
    v_scr[0:BLOCK, :] = v_scr[tq:tq + BLOCK, :]

    y = jnp.dot(att_scr[...], wo_ref[...], preferred_element_type=F32) + bo_ref[...]
    o_ref[...] = x_ref[...] + gate * y


def _slice_cols(head_a, head_b):
    half = HEAD_DIM // 2
    cols = []
    for part in range(2):
        for head in (head_a, head_b):
            start = head * HEAD_DIM + part * half
            cols.extend(range(start, start + half))
    return cols


def _attn_layout():
    q_cols, k_cols, heads = [], [], []
    for sl in range(Q_SLICES):
        pair, g = divmod(sl, STACK)
        a, b = (2 * pair) * GROUP + g, (2 * pair + 1) * GROUP + g
        heads.extend([a, b])
        q_cols.extend(_slice_cols(a, b))
    for sg in range(KV_SLICES):
        k_cols.extend(_slice_cols(2 * sg, 2 * sg + 1))
    as_idx = lambda v: np.asarray(v, np.int32)
    return as_idx(q_cols), as_idx(k_cols), as_idx(heads)


def _slice_gain(gain):
    half = HEAD_DIM // 2
    lo, hi = gain[:, :half], gain[:, half:]
    one_slice = jnp.concatenate([lo, lo, hi, hi], axis=1)
    reps = MXU_COLS // LANES
    return jnp.tile(one_slice, (1, reps)).reshape(gain.shape[0], 1, MXU_COLS)


def _attn_call(x, mod, gains, cos_t, sin_t, wqkv, bqkv, qg, kg, sinks, wo, bo, layer, j, seq):
    n_rows, d = x.shape
    tq = ATTN_TILE
    nb = tq // BLOCK
    x_spec, mod_spec = _token_specs(x, mod, layer, tq, seq)
    rope_spec = pl.BlockSpec((tq, LANES), lambda t: (t, 0))
    return pl.pallas_call(
        functools.partial(_attn_kernel, attn_layer=j, per_seq=seq // tq),
        grid=(n_rows // tq,),
        in_specs=[
            x_spec, mod_spec,
            _resident((None, None, 1, d), (layer, 1)),
            rope_spec, rope_spec,
            _resident((None,) + wqkv.shape[1:], (j,)),
            _resident((None,) + bqkv.shape[1:], (j,)),
            _resident((None,) + qg.shape[1:], (j,)),
            _resident((None,) + kg.shape[1:], (j,)),
            pl.BlockSpec(memory_space=pltpu.SMEM),
            _resident((None,) + wo.shape[1:], (j,)),
            _resident((None,) + bo.shape[1:], (j,)),
        ],
        out_specs=x_spec,
        out_shape=jax.ShapeDtypeStruct(x.shape, F32),
        scratch_shapes=[
            pltpu.VMEM((KV_SLICES, nb, STACK * BLOCK, LANES), BF16),
            pltpu.VMEM((BLOCK + tq, KV_WIDTH), BF16),
            pltpu.VMEM((BLOCK + tq, KV_WIDTH), BF16),
            pltpu.VMEM((tq, d), BF16),
            pltpu.VMEM((2 * KV_SLICES, STACK * BLOCK, LANES), F32),
            pltpu.VMEM((2, 2 * KV_SLICES, STACK * BLOCK, 2 * BLOCK), F32),
        ],
        compiler_params=_TOKEN_PARAMS,
        name=f"attn_l{layer}",
    )(x, mod, gains, cos_t, sin_t, wqkv, bqkv, qg, kg, sinks, wo, bo)


def _conv_kernel(x_ref, mod_ref, gain_ref, win_ref, cw_ref, wout_ref, o_ref, u_scr, *, per_seq):
    tq, d = x_ref.shape
    pad = u_scr.shape[0] - tq

    @pl.when(pl.program_id(0) % per_seq == 0)
    def _():
        u_scr[0:pad, :] = jnp.zeros((pad, d), F32)

    shift, scale, gate = _mod_rows(mod_ref, 1)
    gain = gain_ref[...]
    for r0 in range(0, tq, CONV_ROWS):
        rows = slice(r0, r0 + CONV_ROWS)
        x = x_ref[rows, :]
        h = _mod_norm(x, gain, shift, scale).astype(BF16)
        gc = jnp.dot(h, win_ref[:, d:2 * d], preferred_element_type=F32)
        v = jnp.dot(h, win_ref[:, 2 * d:3 * d], preferred_element_type=F32)
        gb = jnp.dot(h, win_ref[:, 0:d], preferred_element_type=F32)
        u_scr[pad + r0:pad + r0 + CONV_ROWS, :] = gc * v
        conv = cw_ref[CONV_WIDTH - 1:CONV_WIDTH, :] * u_scr[pad + r0:pad + r0 + CONV_ROWS, :]
        for k in range(1, CONV_WIDTH):
            w = cw_ref[CONV_WIDTH - 1 - k:CONV_WIDTH - k, :]
            conv = conv + w * u_scr[pad + r0 - k:pad + r0 - k + CONV_ROWS, :]
        y = jnp.dot((gb * conv).astype(BF16), wout_ref[...], preferred_element_type=F32)
        o_ref[rows, :] = x + gate * y
    u_scr[0:pad, :] = u_scr[tq:tq + pad, :]


def _conv_call(x, mod, gains, win, cw, wout, layer, j, seq):
    n_rows, d = x.shape
    tq = TOKEN_TILE
    x_spec, mod_spec = _token_specs(x, mod, layer, tq, seq)
    return pl.pallas_call(
        functools.partial(_conv_kernel, per_seq=seq // tq),
        grid=(n_rows // tq,),
        in_specs=[
            x_spec, mod_spec,
            _resident((None, None, 1, d), (layer, 1)),
            _resident((None,) + win.shape[1:], (j,)),
            _resident((None,) + cw.shape[1:], (j,)),
            _resident((None,) + wout.shape[1:], (j,)),
        ],
        out_specs=x_spec,
        out_shape=jax.ShapeDtypeStruct(x.shape, F32),
        scratch_shapes=[pltpu.VMEM((tq + 8, d), F32)],
        compiler_params=_TOKEN_PARAMS,
        name=f"conv_l{layer}",
    )(x, mod, gains, win, cw, wout)


def kernel(x, c, positions, norm_gain, w_ada, b_ada, w_ffn_up, w_ffn_down, attn_w_qkv,
           attn_b_qkv, attn_q_gain, attn_k_gain, attn_sinks, attn_w_o, attn_b_o,
           conv_w_in, conv_w, conv_w_out):
    depth = w_ada.shape[0]
    b, s, d = x.shape
    n_ada = w_ada.shape[2] // d
    mod = _ada_call(c, w_ada, b_ada).reshape(depth, b, n_ada, d)
    cos_t, sin_t = _rope_call(positions)
    gains = norm_gain.reshape(depth, -1, 1, d)

    wup = w_ffn_up[0, 0].astype(BF16)
    wdn = w_ffn_down[0, 0].astype(BF16)

    q_cols, k_cols, heads = _attn_layout()
    n_attn = attn_w_qkv.shape[0]
    qkv_cols = np.concatenate([q_cols, d + k_cols, np.arange(d + KV_WIDTH, d + 2 * KV_WIDTH)])
    chunk = HEAD_DIM // 2
    src_chunks = qkv_cols.reshape(-1, chunk)[:, 0] // chunk
    wqkv = attn_w_qkv.reshape(n_attn, d, -1, chunk)[:, :, src_chunks, :]
    wqkv = wqkv.reshape(n_attn, d, -1).astype(BF16)
    bqkv = attn_b_qkv[:, qkv_cols].reshape(n_attn, 1, -1)
    qg = _slice_gain(attn_q_gain)
    kg = _slice_gain(attn_k_gain)
    sinks = attn_sinks[:, heads]
    wo = attn_w_o.reshape(n_attn, N_HEADS, HEAD_DIM, d)[:, heads].reshape(n_attn, d, d).astype(BF16)
    bo = attn_b_o.reshape(n_attn, 1, d)

    win = conv_w_in.astype(BF16)
    wout = conv_w_out.astype(BF16)

    x = x.reshape(b * s, d)
    for i in range(depth):
        x, wup, wdn = _ffn_call(x, mod, gains, wup, wdn, i, 0, s,
                                next_f32=(w_ffn_up, w_ffn_down, (i, 1)))
        j = i // 2
        if i % 2 == 0:
            x = _attn_call(x, mod, gains, cos_t, sin_t, wqkv, bqkv, qg, kg, sinks, wo, bo,
                           i, j, s)
        else:
            x = _conv_call(x, mod, gains, win, conv_w, wout, i, j, s)
        if i + 1 < depth:
            x, wup, wdn = _ffn_call(x, mod, gains, wup, wdn, i, 2, s,
                                    next_f32=(w_ffn_up, w_ffn_down, (i + 1, 0)))
        else:
            x, = _ffn_call(x, mod, gains, wup, wdn, i, 2, s)
    return x.reshape(b, s, d)
```

```python
import functools
import math

import numpy as np
import jax
import jax.numpy as jnp
from jax import lax
from jax.experimental import pallas as pl
from jax.experimental.pallas import tpu as pltpu

F32 = jnp.float32
BF16 = jnp.bfloat16

EPS = 1e-6
ROPE_THETA = 10000.0
N_HEADS = 16
N_KV_HEADS = 4
HEAD_DIM = 64
GROUP = N_HEADS // N_KV_HEADS
BLOCK = 128
CONV_WIDTH = 3
LANES = 128
MXU_COLS = 256
FF_CHUNK = MXU_COLS
TOKEN_TILE = 1024
CONV_ROWS = 512
FFN_TILE = 1024
FFN_ROWS = 512
FFN_HANDOVER_AT = 8
CAST_DOWN_ROWS = 128
ATTN_TILE = 1024
ATTN_BLOCKS_PER_TRIP = 2
ROPE_ROWS = 2048
ADA_COLS = 1536
VMEM_LIMIT = 56 * 1024 * 1024
LOG2E = math.log2(math.e)
KV_WIDTH = N_KV_HEADS * HEAD_DIM
Q_SLICES = N_HEADS * HEAD_DIM // LANES
KV_SLICES = KV_WIDTH // LANES
STACK = Q_SLICES // KV_SLICES


def _resident(shape, index):
    full = tuple(index) + (0,) * (len(shape) - len(index))
    return pl.BlockSpec(shape, lambda *_: full, pipeline_mode=pl.Buffered(1))


def _sigmoid(x):
    return 1.0 / (1.0 + jnp.exp(-x))


def _mod_norm(x, gain, shift, scale):
    ms = jnp.mean(x * x, axis=-1, keepdims=True)
    y = (x * lax.rsqrt(ms + EPS)) * gain
    return y * (1.0 + scale) + shift


def _mod_rows(mod_ref, sub):
    shift = mod_ref[3 * sub + 0:3 * sub + 1, :]
    scale = mod_ref[3 * sub + 1:3 * sub + 2, :]
    gate = mod_ref[3 * sub + 2:3 * sub + 3, :]
    return shift, scale, gate


def _token_specs(x, mod, layer, tile, seq):
    _, d = x.shape
    per_seq = seq // tile
    return (pl.BlockSpec((tile, d), lambda t: (t, 0)),
            pl.BlockSpec((None, None) + mod.shape[2:], lambda t: (layer, t // per_seq, 0, 0)))


_GRID_PARAMS = pltpu.CompilerParams(
    dimension_semantics=("arbitrary", "arbitrary"), vmem_limit_bytes=VMEM_LIMIT)
_TOKEN_PARAMS = pltpu.CompilerParams(
    dimension_semantics=("arbitrary",), vmem_limit_bytes=VMEM_LIMIT)


def _ada_kernel(c_ref, w_ref, b_ref, o_ref):
    c = c_ref[...]
    ca = (c * _sigmoid(c)).astype(BF16)
    o_ref[...] = jnp.dot(ca, w_ref[...].astype(BF16), preferred_element_type=F32) + b_ref[...]


def _ada_call(c, w_ada, b_ada):
    depth, d, n = w_ada.shape
    b = c.shape[0]
    tn = ADA_COLS
    return pl.pallas_call(
        _ada_kernel,
        grid=(depth, n // tn),
        in_specs=[
            pl.BlockSpec((b, d), lambda l, j: (0, 0)),
            pl.BlockSpec((None, d, tn), lambda l, j: (l, 0, j)),
            pl.BlockSpec((None, 1, tn), lambda l, j: (l, 0, j)),
        ],
        out_specs=pl.BlockSpec((None, b, tn), lambda l, j: (l, 0, j)),
        out_shape=jax.ShapeDtypeStruct((depth, b, n), F32),
        compiler_params=_GRID_PARAMS,
        name="ada_mod",
    )(c, w_ada, b_ada.reshape(depth, 1, n))


def _rope_kernel(pos_ref, invf_ref, sign_ref, cos_ref, sin_ref):
    ang = pos_ref[...].astype(F32) * invf_ref[...]
    per_row = LANES // (HEAD_DIM // 2)
    rows = ang.shape[0]
    group = lax.broadcasted_iota(jnp.int32, ang.shape, 1) // (HEAD_DIM // 2)
    for table, out_ref, sign in ((jnp.cos(ang), cos_ref, None), (jnp.sin(ang), sin_ref, sign_ref)):
        for m in range(per_row):
            one = jnp.where(group == m, table, 0.0)
            one = one + pltpu.roll(one, HEAD_DIM // 2, 1)
            one = one + pltpu.roll(one, HEAD_DIM, 1)
            if sign is not None:
                one = one * sign[...]
            out_ref[pl.ds(m, rows, stride=per_row), :] = one


def _rope_call(positions):
    b, s = positions.shape
    half = HEAD_DIM // 2
    per_row = LANES // half
    rows = b * s // per_row
    inv_freq = ROPE_THETA ** (-jnp.arange(0, HEAD_DIM, 2, dtype=F32) / HEAD_DIM)
    invf = jnp.tile(inv_freq, per_row).reshape(1, LANES)
    sign = jnp.where(jnp.arange(LANES) < LANES // 2, -1.0, 1.0).astype(F32).reshape(1, LANES)
    pos_rep = jnp.repeat(positions.reshape(rows, per_row), half, axis=1)
    tr = min(ROPE_ROWS, rows)
    const = pl.BlockSpec((1, LANES), lambda i: (0, 0))
    out_spec = pl.BlockSpec((tr * per_row, LANES), lambda i: (i, 0))
    return pl.pallas_call(
        _rope_kernel,
        grid=(rows // tr,),
        in_specs=[pl.BlockSpec((tr, LANES), lambda i: (i, 0)), const, const],
        out_specs=[out_spec, out_spec],
        out_shape=[jax.ShapeDtypeStruct((b * s, LANES), F32)] * 2,
        compiler_params=pltpu.CompilerParams(dimension_semantics=("arbitrary",)),
        name="rope_tables",
    )(pos_rep, invf, sign)


def _ffn_kernel(x_ref, xn_ref, mod_ref, modn_ref, gain_ref, wup_ref, wdn_ref, *rest,
                sub, cast_next):
    if cast_next:
        wup_f32, wdn_f32, o_ref, wup_next, wdn_next, h_scr, a_scr = rest
        wup_next[...] = wup_f32[...].astype(BF16)
        wdn_next[...] = wdn_f32[...].astype(BF16)
    else:
        o_ref, h_scr, a_scr = rest
    shift, scale, gate = _mod_rows(mod_ref, sub)
    shift_n, scale_n, _ = _mod_rows(modn_ref, sub)
    gain = gain_ref[...]
    d_ff = wdn_ref.shape[0]
    n_groups = h_scr.shape[0]
    n_chunks = d_ff // FF_CHUNK
    rows_of = lambda g: slice(g * FFN_ROWS, (g + 1) * FFN_ROWS)

    def prepare(g):
        if g < n_groups:
            h = _mod_norm(x_ref[rows_of(g), :], gain, shift, scale)
        else:
            h = _mod_norm(xn_ref[...], gain, shift_n, scale_n)
        h_scr[g % n_groups] = h.astype(BF16)

    def up_chunk(g, c):
        lo = c * FF_CHUNK
        h = h_scr[g]
        gg = jnp.dot(h, wup_ref[:, lo:lo + FF_CHUNK], preferred_element_type=F32)
        uu = jnp.dot(h, wup_ref[:, d_ff + lo:d_ff + lo + FF_CHUNK], preferred_element_type=F32)
        a_scr[rows_of(g), lo:lo + FF_CHUNK] = ((gg * _sigmoid(gg)) * uu).astype(BF16)

    @pl.when(pl.program_id(0) == 0)
    def _():
        prepare(0)
        up_chunk(0, 0)

    for g in range(n_groups):
        rows = rows_of(g)
        for c in range(1, n_chunks):
            up_chunk(g, c)
            if c == FFN_HANDOVER_AT:
                prepare(g + 1)
                up_chunk((g + 1) % n_groups, 0)
        y = jnp.dot(a_scr[rows, :], wdn_ref[...], preferred_element_type=F32)
        o_ref[rows, :] = x_ref[rows, :] + (0.5 * gate) * y


def _ffn_call(x, mod, gains, wup, wdn, layer, sub, seq, next_f32=None):
    n_rows, d = x.shape
    ts = FFN_TILE
    d_ff = wdn.shape[0]
    n_tiles = n_rows // ts
    groups = ts // FFN_ROWS
    per_seq = seq // ts
    x_spec, mod_spec = _token_specs(x, mod, layer, ts, seq)
    nxt = lambda t: jnp.minimum(t + 1, n_tiles - 1)
    xn_spec = pl.BlockSpec((FFN_ROWS, d), lambda t: (nxt(t) * groups, 0))
    modn_spec = pl.BlockSpec((None, None) + mod.shape[2:],
                             lambda t: (layer, nxt(t) // per_seq, 0, 0))
    in_specs = [x_spec, xn_spec, mod_spec, modn_spec,
                _resident((None, None, 1, d), (layer, sub)),
                _resident(wup.shape, ()), _resident(wdn.shape, ())]
    operands = [x, x, mod, mod, gains, wup, wdn]
    out_specs = [x_spec]
    out_shape = [jax.ShapeDtypeStruct(x.shape, F32)]
    if next_f32 is not None:
        w_up_all, w_down_all, (nl, nw) = next_f32
        up_rows = d // n_tiles
        dn_rows = CAST_DOWN_ROWS
        dn_last = d_ff // dn_rows - 1
        in_specs += [
            pl.BlockSpec((None, None, up_rows, 2 * d_ff), lambda t: (nl, nw, t, 0)),
            pl.BlockSpec((None, None, dn_rows, d), lambda t: (nl, nw, jnp.minimum(t, dn_last), 0)),
        ]
        operands += [w_up_all, w_down_all]
        out_specs += [
            pl.BlockSpec((up_rows, 2 * d_ff), lambda t: (t, 0)),
            pl.BlockSpec((dn_rows, d), lambda t: (jnp.minimum(t, dn_last), 0)),
        ]
        out_shape += [jax.ShapeDtypeStruct((d, 2 * d_ff), BF16),
                      jax.ShapeDtypeStruct((d_ff, d), BF16)]
    return pl.pallas_call(
        functools.partial(_ffn_kernel, sub=sub, cast_next=next_f32 is not None),
        grid=(n_tiles,),
        in_specs=in_specs,
        out_specs=out_specs,
        out_shape=out_shape,
        scratch_shapes=[pltpu.VMEM((groups, FFN_ROWS, d), BF16), pltpu.VMEM((ts, d_ff), BF16)],
        compiler_params=_TOKEN_PARAMS,
        name=f"ffn_l{layer}_s{sub}",
    )(*operands)


def _norm_rope(t, ms, gain, cos, sin):
    tn = (t * lax.rsqrt(ms + EPS)) * gain
    outs = []
    for j in range(t.shape[1] // LANES):
        v = tn[:, j * LANES:(j + 1) * LANES]
        outs.append(v * cos + pltpu.roll(v, LANES // 2, 1) * sin)
    return outs


def _attn_kernel(x_ref, mod_ref, gain_ref, cos_ref, sin_ref, wqkv_ref,
                 bqkv_ref, qg_ref, kg_ref, sink_ref, wo_ref, bo_ref, o_ref,
                 q_scr, k_scr, v_scr, att_scr, sink_scr, sc_scr, *, attn_layer, per_seq):
    tq, d = x_ref.shape
    nb = tq // BLOCK
    rows_stack = STACK * BLOCK
    first = pl.program_id(0) % per_seq == 0

    @pl.when(first)
    def _():
        k_scr[0:BLOCK, :] = jnp.zeros((BLOCK, KV_WIDTH), BF16)
        v_scr[0:BLOCK, :] = jnp.zeros((BLOCK, KV_WIDTH), BF16)

    @pl.when(pl.program_id(0) == 0)
    def _():
        for sg in range(KV_SLICES):
            for half in range(2):
                for g in range(STACK):
                    val = sink_ref[attn_layer, 2 * (sg * STACK + g) + half] * LOG2E
                    sink_scr[2 * sg + half, g * BLOCK:(g + 1) * BLOCK, :] = jnp.full(
                        (BLOCK, LANES), val, F32)

    shift, scale, gate = _mod_rows(mod_ref, 1)
    h = _mod_norm(x_ref[...], gain_ref[...], shift, scale).astype(BF16)

    cos = cos_ref[...]
    sin = sin_ref[...]

    def head_of(col):
        return (col // LANES) * 2 + (col // (HEAD_DIM // 2)) % 2

    r = head_of(lax.broadcasted_iota(jnp.int32, (MXU_COLS, MXU_COLS), 0))
    c = head_of(lax.broadcasted_iota(jnp.int32, (MXU_COLS, MXU_COLS), 1))
    group_mean = jnp.where(r == c, 1.0 / HEAD_DIM, 0.0).astype(BF16)

    n_chunks = (d + 2 * KV_WIDTH) // MXU_COLS
    n_norm = (d + KV_WIDTH) // MXU_COLS
    qgain = qg_ref[...] * (HEAD_DIM ** -0.5 * LOG2E)

    def project(cc):
        cols = slice(cc * MXU_COLS, (cc + 1) * MXU_COLS)
        return jnp.dot(h, wqkv_ref[:, cols], preferred_element_type=F32) + bqkv_ref[:, cols]

    def finish(cc, t):
        if cc == n_norm:
            v_scr[BLOCK:, :] = t.astype(BF16)
            return
        ms = jnp.dot((t * t).astype(BF16), group_mean, preferred_element_type=F32)
        is_q = cc < d // MXU_COLS
        outs = _norm_rope(t, ms, qgain if is_q else kg_ref[...], cos, sin)
        for j, o in enumerate(outs):
            ob = o.astype(BF16)
            if is_q:
                sg, g = divmod(cc * (MXU_COLS // LANES) + j, STACK)
                for n in range(nb):
                    q_scr[sg, n, g * BLOCK:(g + 1) * BLOCK, :] = ob[n * BLOCK:(n + 1) * BLOCK]
            else:
                k_scr[BLOCK:, j * LANES:(j + 1) * LANES] = ob

    ahead = project(0)
    for cc in range(n_chunks):
        t = ahead
        if cc + 1 < n_chunks:
            ahead = project(cc + 1)
        finish(cc, t)

    qi = lax.broadcasted_iota(jnp.int32, (rows_stack, LANES), 0) % BLOCK
    kc = lax.broadcasted_iota(jnp.int32, (rows_stack, LANES), 1)
    take_prev = kc > qi
    low_q = kc < HEAD_DIM
    kv_lane = lax.broadcasted_iota(jnp.int32, (2 * BLOCK, LANES), 1)
    k_first = (kv_lane // (HEAD_DIM // 2)) % 2 == 0
    v_first = kv_lane < HEAD_DIM
    no_prev = jnp.where(first, -jnp.inf, 0.0)
    zero_k = jnp.zeros((2 * BLOCK, LANES), BF16)
    one_k = jnp.ones((2 * BLOCK, LANES), BF16)
    zero_p = jnp.zeros((rows_stack, LANES), F32)

    def block_rows(n, count):
        start = n * BLOCK if isinstance(n, int) else pl.multiple_of(n * BLOCK, BLOCK)
        return pl.ds(start, count * BLOCK)

    units = [(sg, half) for sg in range(KV_SLICES) for half in range(2)]

    def score_tile(n, unit):
        sg, half = unit
        kb = k_scr[block_rows(n, 2), sg * LANES:(sg + 1) * LANES]
        kk = jnp.where(k_first, kb, zero_k) if half == 0 else jnp.where(k_first, zero_k, kb)
        sc = lax.dot_general(q_scr[sg, n], kk, (((1,), (1,)), ((), ())),
                             preferred_element_type=F32)
        bias = jnp.where(n == 0, no_prev, 0.0)
        return jnp.where(take_prev, sc[:, :BLOCK] + bias, sc[:, BLOCK:])

    def weighted_values(n, slot, unit):
        sg, half = unit
        s = sc_scr[slot, 2 * sg + half]
        sink = sink_scr[2 * sg + half]
        m = jnp.maximum(jnp.max(s, axis=-1, keepdims=True), sink)
        p = jnp.exp2(s - m)
        p_cat = jnp.concatenate([jnp.where(take_prev, p, zero_p).astype(BF16),
                                 jnp.where(take_prev, zero_p, p).astype(BF16)], axis=1)
        vb = v_scr[block_rows(n, 2), sg * LANES:(sg + 1) * LANES]
        vv = jnp.where(v_first, vb, one_k) if half == 0 else jnp.where(v_first, one_k, vb)
        o = jnp.dot(p_cat, vv, preferred_element_type=F32)
        return o, o + jnp.exp2(sink - m)

    def combine(n, sg, lo, hi):
        (o_lo, d_lo), (o_hi, d_hi) = lo, hi
        numer = jnp.where(low_q, o_lo, o_hi)
        den = pltpu.roll(jnp.where(low_q, d_hi, d_lo), HEAD_DIM, 1)
        out = (numer * (1.0 / den)).astype(BF16)
        for g in range(STACK):
            sl = sg * STACK + g
            att_scr[block_rows(n, 1), sl * LANES:(sl + 1) * LANES] = out[g * BLOCK:(g + 1) * BLOCK]

    def attend(n, slot, ahead_of=None):
        ahead, res = [], []
        for unit in units:
            if ahead_of is not None:
                ahead.append(score_tile(ahead_of, unit))
            res.append(weighted_values(n, slot, unit))
            if unit[1] == 1:
                combine(n, unit[0], *res)
                res = []
        return ahead

    def stage(slot, tiles):
        for u, tile in enumerate(tiles):
            sc_scr[slot, u] = tile

    per_trip = sc_scr.shape[0]
    for slot in range(per_trip):
        stage(slot, [score_tile(slot, unit) for unit in units])

    n_trips = nb // per_trip

    def out_proj(i):
        rows = block_rows(i * per_trip, per_trip)
        y = jnp.dot(att_scr[rows, :], wo_ref[...], preferred_element_type=F32) + bo_ref[...]
        o_ref[rows, :] = x_ref[rows, :] + gate * y

    def trip(i, last=False):
        for slot in range(per_trip):
            n = i * per_trip + slot
            ahead = attend(n, slot, ahead_of=None if last else n + per_trip)
            if not last:
                stage(slot, ahead)

    trip(0)

    def body(i, carry):
        out_proj(i - 1)
        trip(i)
        return carry

    lax.fori_loop(1, n_trips - 1, body, 0)
    out_proj(n_trips - 2)
    trip(n_trips - 1, last=True)
    out_proj(n_trips - 1)

    k_scr[0:BLOCK, :] = k_scr[tq:tq + BLOCK, :]
    v_scr[0:BLOCK, :] = v_scr[tq:tq + BLOCK, :]


def _slice_cols(head_a, head_b):
    half = HEAD_DIM // 2
    cols = []
    for part in range(2):
        for head in (head_a, head_b):
            start = head * HEAD_DIM + part * half
            cols.extend(range(start, start + half))
    return cols


def _attn_layout():
    q_cols, k_cols, heads = [], [], []
    for sl in range(Q_SLICES):
        pair, g = divmod(sl, STACK)
        a, b = (2 * pair) * GROUP + g, (2 * pair + 1) * GROUP + g
        heads.extend([a, b])
        q_cols.extend(_slice_cols(a, b))
    for sg in range(KV_SLICES):
        k_cols.extend(_slice_cols(2 * sg, 2 * sg + 1))
    as_idx = lambda v: np.asarray(v, np.int32)
    return as_idx(q_cols), as_idx(k_cols), as_idx(heads)


def _slice_gain(gain):
    half = HEAD_DIM // 2
    lo, hi = gain[:, :half], gain[:, half:]
    one_slice = jnp.concatenate([lo, lo, hi, hi], axis=1)
    reps = MXU_COLS // LANES
    return jnp.tile(one_slice, (1, reps)).reshape(gain.shape[0], 1, MXU_COLS)


def _attn_call(x, mod, gains, cos_t, sin_t, wqkv, bqkv, qg, kg, sinks, wo, bo, layer, j, seq):
    n_rows, d = x.shape
    tq = ATTN_TILE
    nb = tq // BLOCK
    x_spec, mod_spec = _token_specs(x, mod, layer, tq, seq)
    rope_spec = pl.BlockSpec((tq, LANES), lambda t: (t, 0))
    return pl.pallas_call(
        functools.partial(_attn_kernel, attn_layer=j, per_seq=seq // tq),
        grid=(n_rows // tq,),
        in_specs=[
            x_spec, mod_spec,
            _resident((None, None, 1, d), (layer, 1)),
            rope_spec, rope_spec,
            _resident((None,) + wqkv.shape[1:], (j,)),
            _resident((None,) + bqkv.shape[1:], (j,)),
            _resident((None,) + qg.shape[1:], (j,)),
            _resident((None,) + kg.shape[1:], (j,)),
            pl.BlockSpec(memory_space=pltpu.SMEM),
            _resident((None,) + wo.shape[1:], (j,)),
            _resident((None,) + bo.shape[1:], (j,)),
        ],
        out_specs=x_spec,
        out_shape=jax.ShapeDtypeStruct(x.shape, F32),
        scratch_shapes=[
            pltpu.VMEM((KV_SLICES, nb, STACK * BLOCK, LANES), BF16),
            pltpu.VMEM((BLOCK + tq, KV_WIDTH), BF16),
            pltpu.VMEM((BLOCK + tq, KV_WIDTH), BF16),
            pltpu.VMEM((tq, d), BF16),
            pltpu.VMEM((2 * KV_SLICES, STACK * BLOCK, LANES), F32),
            pltpu.VMEM((ATTN_BLOCKS_PER_TRIP, 2 * KV_SLICES, STACK * BLOCK, BLOCK), F32),
        ],
        compiler_params=_TOKEN_PARAMS,
        name=f"attn_l{layer}",
    )(x, mod, gains, cos_t, sin_t, wqkv, bqkv, qg, kg, sinks, wo, bo)


def _conv_kernel(x_ref, mod_ref, gain_ref, win_ref, cw_ref, wout_ref, o_ref, u_scr, *, per_seq):
    tq, d = x_ref.shape
    pad = u_scr.shape[0] - tq

    @pl.when(pl.program_id(0) % per_seq == 0)
    def _():
        u_scr[0:pad, :] = jnp.zeros((pad, d), F32)

    shift, scale, gate = _mod_rows(mod_ref, 1)
    gain = gain_ref[...]
    for r0 in range(0, tq, CONV_ROWS):
        rows = slice(r0, r0 + CONV_ROWS)
        x = x_ref[rows, :]
        h = _mod_norm(x, gain, shift, scale).astype(BF16)
        gc = jnp.dot(h, win_ref[:, d:2 * d], preferred_element_type=F32)
        v = jnp.dot(h, win_ref[:, 2 * d:3 * d], preferred_element_type=F32)
        gb = jnp.dot(h, win_ref[:, 0:d], preferred_element_type=F32)
        u_scr[pad + r0:pad + r0 + CONV_ROWS, :] = gc * v
        conv = cw_ref[CONV_WIDTH - 1:CONV_WIDTH, :] * u_scr[pad + r0:pad + r0 + CONV_ROWS, :]
        for k in range(1, CONV_WIDTH):
            w = cw_ref[CONV_WIDTH - 1 - k:CONV_WIDTH - k, :]
            conv = conv + w * u_scr[pad + r0 - k:pad + r0 - k + CONV_ROWS, :]
        y = jnp.dot((gb * conv).astype(BF16), wout_ref[...], preferred_element_type=F32)
        o_ref[rows, :] = x + gate * y
    u_scr[0:pad, :] = u_scr[tq:tq + pad, :]


def _conv_call(x, mod, gains, win, cw, wout, layer, j, seq):
    n_rows, d = x.shape
    tq = TOKEN_TILE
    x_spec, mod_spec = _token_specs(x, mod, layer, tq, seq)
    return pl.pallas_call(
        functools.partial(_conv_kernel, per_seq=seq // tq),
        grid=(n_rows // tq,),
        in_specs=[
            x_spec, mod_spec,
            _resident((None, None, 1, d), (layer, 1)),
            _resident((None,) + win.shape[1:], (j,)),
            _resident((None,) + cw.shape[1:], (j,)),
            _resident((None,) + wout.shape[1:], (j,)),
        ],
        out_specs=x_spec,
        out_shape=jax.ShapeDtypeStruct(x.shape, F32),
        scratch_shapes=[pltpu.VMEM((tq + 8, d), F32)],
        compiler_params=_TOKEN_PARAMS,
        name=f"conv_l{layer}",
    )(x, mod, gains, win, cw, wout)


def kernel(x, c, positions, norm_gain, w_ada, b_ada, w_ffn_up, w_ffn_down, attn_w_qkv,
           attn_b_qkv, attn_q_gain, attn_k_gain, attn_sinks, attn_w_o, attn_b_o,
           conv_w_in, conv_w, conv_w_out):
    depth = w_ada.shape[0]
    b, s, d = x.shape
    n_ada = w_ada.shape[2] // d
    mod = _ada_call(c, w_ada, b_ada).reshape(depth, b, n_ada, d)
    cos_t, sin_t = _rope_call(positions)
    gains = norm_gain.reshape(depth, -1, 1, d)

    wup = w_ffn_up[0, 0].astype(BF16)
    wdn = w_ffn_down[0, 0].astype(BF16)

    q_cols, k_cols, heads = _attn_layout()
    n_attn = attn_w_qkv.shape[0]
    qkv_cols = np.concatenate([q_cols, d + k_cols, np.arange(d + KV_WIDTH, d + 2 * KV_WIDTH)])
    chunk = HEAD_DIM // 2
    src_chunks = qkv_cols.reshape(-1, chunk)[:, 0] // chunk
    wqkv = attn_w_qkv.reshape(n_attn, d, -1, chunk)[:, :, src_chunks, :]
    wqkv = wqkv.reshape(n_attn, d, -1).astype(BF16)
    bqkv = attn_b_qkv[:, qkv_cols].reshape(n_attn, 1, -1)
    qg = _slice_gain(attn_q_gain)
    kg = _slice_gain(attn_k_gain)
    sinks = attn_sinks[:, heads]
    wo = attn_w_o.reshape(n_attn, N_HEADS, HEAD_DIM, d)[:, heads].reshape(n_attn, d, d).astype(BF16)
    bo = attn_b_o.reshape(n_attn, 1, d)

    win = conv_w_in.astype(BF16)
    wout = conv_w_out.astype(BF16)

    x = x.reshape(b * s, d)
    for i in range(depth):
        x, wup, wdn = _ffn_call(x, mod, gains, wup, wdn, i, 0, s,
                                next_f32=(w_ffn_up, w_ffn_down, (i, 1)))
        j = i // 2
        if i % 2 == 0:
            x = _attn_call(x, mod, gains, cos_t, sin_t, wqkv, bqkv, qg, kg, sinks, wo, bo,
                           i, j, s)
        else:
            x = _conv_call(x, mod, gains, win, conv_w, wout, i, j, s)
        if i + 1 < depth:
            x, wup, wdn = _ffn_call(x, mod, gains, wup, wdn, i, 2, s,
                                    next_f32=(w_ffn_up, w_ffn_down, (i + 1, 0)))
        else:
            x, = _ffn_call(x, mod, gains, wup, wdn, i, 2, s)
    return x.reshape(b, s, d)
```

```python
import functools
import math

import numpy as np
import jax
import jax.numpy as jnp
from jax import lax
from jax.experimental import pallas as pl
from jax.experimental.pallas import tpu as pltpu

F32 = jnp.float32
BF16 = jnp.bfloat16

EPS = 1e-6
ROPE_THETA = 10000.0
N_HEADS = 16
N_KV_HEADS = 4
HEAD_DIM = 64
GROUP = N_HEADS // N_KV_HEADS
BLOCK = 128
CONV_WIDTH = 3
LANES = 128
MXU_COLS = 256
FF_CHUNK = MXU_COLS
TOKEN_TILE = 1024
CONV_ROWS = 512
FFN_TILE = 1024
FFN_ROWS = 512
CAST_ROW_ALIGN = 16
CAST_DOWN_ROWS = 128
ATTN_TILE = 1024
ATTN_BLOCKS_PER_TRIP = 2
ROPE_ROWS = 2048
ADA_COLS = 1536
VMEM_LIMIT = 56 * 1024 * 1024
LOG2E = math.log2(math.e)
KV_WIDTH = N_KV_HEADS * HEAD_DIM
Q_SLICES = N_HEADS * HEAD_DIM // LANES
KV_SLICES = KV_WIDTH // LANES
STACK = Q_SLICES // KV_SLICES


def _resident(shape, index):
    full = tuple(index) + (0,) * (len(shape) - len(index))
    return pl.BlockSpec(shape, lambda *_: full, pipeline_mode=pl.Buffered(1))


def _sigmoid(x):
    return 1.0 / (1.0 + jnp.exp(-x))


def _mod_norm(x, gain, shift, scale):
    ms = jnp.mean(x * x, axis=-1, keepdims=True)
    return (x * lax.rsqrt(ms + EPS)) * (gain * (1.0 + scale)) + shift


def _mod_rows(mod_ref, sub):
    shift = mod_ref[3 * sub + 0:3 * sub + 1, :]
    scale = mod_ref[3 * sub + 1:3 * sub + 2, :]
    gate = mod_ref[3 * sub + 2:3 * sub + 3, :]
    return shift, scale, gate


def _token_specs(x, mod, layer, tile, seq):
    _, d = x.shape
    per_seq = seq // tile
    return (pl.BlockSpec((tile, d), lambda t: (t, 0)),
            pl.BlockSpec((None, None) + mod.shape[2:], lambda t: (layer, t // per_seq, 0, 0)))


_GRID_PARAMS = pltpu.CompilerParams(
    dimension_semantics=("arbitrary", "arbitrary"), vmem_limit_bytes=VMEM_LIMIT)
_TOKEN_PARAMS = pltpu.CompilerParams(
    dimension_semantics=("arbitrary",), vmem_limit_bytes=VMEM_LIMIT)


def _ada_kernel(c_ref, w_ref, b_ref, o_ref):
    c = c_ref[...]
    ca = (c * _sigmoid(c)).astype(BF16)
    o_ref[...] = jnp.dot(ca, w_ref[...].astype(BF16), preferred_element_type=F32) + b_ref[...]


def _ada_call(c, w_ada, b_ada):
    depth, d, n = w_ada.shape
    b = c.shape[0]
    tn = ADA_COLS
    return pl.pallas_call(
        _ada_kernel,
        grid=(depth, n // tn),
        in_specs=[
            pl.BlockSpec((b, d), lambda l, j: (0, 0)),
            pl.BlockSpec((None, d, tn), lambda l, j: (l, 0, j)),
            pl.BlockSpec((None, 1, tn), lambda l, j: (l, 0, j)),
        ],
        out_specs=pl.BlockSpec((None, b, tn), lambda l, j: (l, 0, j)),
        out_shape=jax.ShapeDtypeStruct((depth, b, n), F32),
        compiler_params=_GRID_PARAMS,
        name="ada_mod",
    )(c, w_ada, b_ada.reshape(depth, 1, n))


def _rope_kernel(pos_ref, invf_ref, sign_ref, cos_ref, sin_ref):
    ang = pos_ref[...].astype(F32) * invf_ref[...]
    per_row = LANES // (HEAD_DIM // 2)
    rows = ang.shape[0]
    group = lax.broadcasted_iota(jnp.int32, ang.shape, 1) // (HEAD_DIM // 2)
    for table, out_ref, sign in ((jnp.cos(ang), cos_ref, None), (jnp.sin(ang), sin_ref, sign_ref)):
        for m in range(per_row):
            one = jnp.where(group == m, table, 0.0)
            one = one + pltpu.roll(one, HEAD_DIM // 2, 1)
            one = one + pltpu.roll(one, HEAD_DIM, 1)
            if sign is not None:
                one = one * sign[...]
            out_ref[pl.ds(m, rows, stride=per_row), :] = one


def _rope_call(positions):
    b, s = positions.shape
    half = HEAD_DIM // 2
    per_row = LANES // half
    rows = b * s // per_row
    inv_freq = ROPE_THETA ** (-jnp.arange(0, HEAD_DIM, 2, dtype=F32) / HEAD_DIM)
    invf = jnp.tile(inv_freq, per_row).reshape(1, LANES)
    sign = jnp.where(jnp.arange(LANES) < LANES // 2, -1.0, 1.0).astype(F32).reshape(1, LANES)
    pos_rep = jnp.repeat(positions.reshape(rows, per_row), half, axis=1)
    tr = min(ROPE_ROWS, rows)
    const = pl.BlockSpec((1, LANES), lambda i: (0, 0))
    out_spec = pl.BlockSpec((tr * per_row, LANES), lambda i: (i, 0))
    return pl.pallas_call(
        _rope_kernel,
        grid=(rows // tr,),
        in_specs=[pl.BlockSpec((tr, LANES), lambda i: (i, 0)), const, const],
        out_specs=[out_spec, out_spec],
        out_shape=[jax.ShapeDtypeStruct((b * s, LANES), F32)] * 2,
        compiler_params=pltpu.CompilerParams(dimension_semantics=("arbitrary",)),
        name="rope_tables",
    )(pos_rep, invf, sign)


def _ffn_kernel(x_ref, xn_ref, mod_ref, modn_ref, gain_ref, wup_ref, wdn_ref, *rest,
                sub, n_casts):
    cast_src, (o_ref, *cast_dst), (h_scr, a_scr) = (
        rest[:n_casts], rest[n_casts:2 * n_casts + 1], rest[2 * n_casts + 1:])
    for src, dst in zip(cast_src, cast_dst):
        dst[...] = src[...].astype(BF16)
    shift, scale, gate = _mod_rows(mod_ref, sub)
    shift_n, scale_n, _ = _mod_rows(modn_ref, sub)
    gain = gain_ref[...]
    d_ff = wdn_ref.shape[0]
    n_groups = h_scr.shape[0]
    n_chunks = d_ff // FF_CHUNK
    rows_of = lambda g: slice(g * FFN_ROWS, (g + 1) * FFN_ROWS)

    def prepare(g):
        if g < n_groups:
            h = _mod_norm(x_ref[rows_of(g), :], gain, shift, scale)
        else:
            h = _mod_norm(xn_ref[...], gain, shift_n, scale_n)
        h_scr[g % n_groups] = h.astype(BF16)

    def up_chunk(g, c):
        lo = c * FF_CHUNK
        h = h_scr[g]
        gg = jnp.dot(h, wup_ref[:, lo:lo + FF_CHUNK], preferred_element_type=F32)
        uu = jnp.dot(h, wup_ref[:, d_ff + lo:d_ff + lo + FF_CHUNK], preferred_element_type=F32)
        a_scr[rows_of(g), lo:lo + FF_CHUNK] = ((gg * _sigmoid(gg)) * uu).astype(BF16)

    @pl.when(pl.program_id(0) == 0)
    def _():
        prepare(0)
        up_chunk(0, 0)

    def down(g, cols):
        rows = rows_of(g)
        y = jnp.dot(a_scr[rows, :], wdn_ref[:, cols], preferred_element_type=F32)
        o_ref[rows, cols] = x_ref[rows, cols] + (0.5 * gate[:, cols]) * y

    d = x_ref.shape[1]
    for g in range(n_groups):
        for c in range(1, n_chunks):
            up_chunk(g, c)
        down(g, slice(0, d // 2))
        prepare(g + 1)
        up_chunk((g + 1) % n_groups, 0)
        down(g, slice(d // 2, d))


def _ffn_call(x, mod, gains, wup, wdn, layer, sub, seq, casts=()):
    n_rows, d = x.shape
    ts = FFN_TILE
    d_ff = wdn.shape[0]
    n_tiles = n_rows // ts
    groups = ts // FFN_ROWS
    per_seq = seq // ts
    x_spec, mod_spec = _token_specs(x, mod, layer, ts, seq)
    nxt = lambda t: jnp.minimum(t + 1, n_tiles - 1)
    xn_spec = pl.BlockSpec((FFN_ROWS, d), lambda t: (nxt(t) * groups, 0))
    modn_spec = pl.BlockSpec((None, None) + mod.shape[2:],
                             lambda t: (layer, nxt(t) // per_seq, 0, 0))
    in_specs = [x_spec, xn_spec, mod_spec, modn_spec,
                _resident((None, None, 1, d), (layer, sub)),
                _resident(wup.shape, ()), _resident(wdn.shape, ())]
    operands = [x, x, mod, mod, gains, wup, wdn]
    out_specs = [x_spec]
    out_shape = [jax.ShapeDtypeStruct(x.shape, F32)]
    for stacked, lead in casts:
        n_r, n_c = stacked.shape[-2:]
        rows = n_r // n_tiles if n_r % (n_tiles * CAST_ROW_ALIGN) == 0 else CAST_DOWN_ROWS
        last = n_r // rows - 1
        blk = lambda t, last=last: jnp.minimum(t, last)
        in_specs.append(pl.BlockSpec((None,) * len(lead) + (rows, n_c),
                                     lambda t, lead=lead, blk=blk: lead + (blk(t), 0)))
        operands.append(stacked)
        out_specs.append(pl.BlockSpec((rows, n_c), lambda t, blk=blk: (blk(t), 0)))
        out_shape.append(jax.ShapeDtypeStruct((n_r, n_c), BF16))
    return pl.pallas_call(
        functools.partial(_ffn_kernel, sub=sub, n_casts=len(casts)),
        grid=(n_tiles,),
        in_specs=in_specs,
        out_specs=out_specs,
        out_shape=out_shape,
        scratch_shapes=[pltpu.VMEM((groups, FFN_ROWS, d), BF16), pltpu.VMEM((ts, d_ff), BF16)],
        compiler_params=_TOKEN_PARAMS,
        name=f"ffn_l{layer}_s{sub}",
    )(*operands)


def _norm_rope(t, ms, gain, cos, sin):
    tn = (t * lax.rsqrt(ms + EPS)) * gain
    outs = []
    for j in range(t.shape[1] // LANES):
        v = tn[:, j * LANES:(j + 1) * LANES]
        outs.append(v * cos + pltpu.roll(v, LANES // 2, 1) * sin)
    return outs


def _attn_kernel(x_ref, mod_ref, gain_ref, cos_ref, sin_ref, wqkv_ref,
                 bqkv_ref, qg_ref, kg_ref, sink_ref, wo_ref, bo_ref, o_ref,
                 q_scr, k_scr, v_scr, att_scr, sink_scr, sc_scr, *, attn_layer, per_seq):
    tq, d = x_ref.shape
    nb = tq // BLOCK
    rows_stack = STACK * BLOCK
    first = pl.program_id(0) % per_seq == 0

    @pl.when(first)
    def _():
        k_scr[0:BLOCK, :] = jnp.zeros((BLOCK, KV_WIDTH), BF16)
        v_scr[0:BLOCK, :] = jnp.zeros((BLOCK, KV_WIDTH), BF16)

    @pl.when(pl.program_id(0) == 0)
    def _():
        for sg in range(KV_SLICES):
            for half in range(2):
                for g in range(STACK):
                    val = sink_ref[attn_layer, 2 * (sg * STACK + g) + half] * LOG2E
                    sink_scr[2 * sg + half, g * BLOCK:(g + 1) * BLOCK, :] = jnp.full(
                        (BLOCK, LANES), val, F32)

    shift, scale, gate = _mod_rows(mod_ref, 1)
    h = _mod_norm(x_ref[...], gain_ref[...], shift, scale).astype(BF16)

    cos = cos_ref[...]
    sin = sin_ref[...]

    def head_of(col):
        return (col // LANES) * 2 + (col // (HEAD_DIM // 2)) % 2

    r = head_of(lax.broadcasted_iota(jnp.int32, (MXU_COLS, MXU_COLS), 0))
    c = head_of(lax.broadcasted_iota(jnp.int32, (MXU_COLS, MXU_COLS), 1))
    group_mean = jnp.where(r == c, 1.0 / HEAD_DIM, 0.0).astype(BF16)

    n_chunks = (d + 2 * KV_WIDTH) // MXU_COLS
    n_norm = (d + KV_WIDTH) // MXU_COLS
    qgain = qg_ref[...] * (HEAD_DIM ** -0.5 * LOG2E)

    def project(cc):
        cols = slice(cc * MXU_COLS, (cc + 1) * MXU_COLS)
        return jnp.dot(h, wqkv_ref[:, cols], preferred_element_type=F32) + bqkv_ref[:, cols]

    def finish(cc, t):
        if cc == n_norm:
            v_scr[BLOCK:, :] = t.astype(BF16)
            return
        ms = jnp.dot((t * t).astype(BF16), group_mean, preferred_element_type=F32)
        is_q = cc < d // MXU_COLS
        outs = _norm_rope(t, ms, qgain if is_q else kg_ref[...], cos, sin)
        for j, o in enumerate(outs):
            ob = o.astype(BF16)
            if is_q:
                sg, g = divmod(cc * (MXU_COLS // LANES) + j, STACK)
                for n in range(nb):
                    q_scr[sg, n, g * BLOCK:(g + 1) * BLOCK, :] = ob[n * BLOCK:(n + 1) * BLOCK]
            else:
                k_scr[BLOCK:, j * LANES:(j + 1) * LANES] = ob

    ahead = project(0)
    for cc in range(n_chunks):
        t = ahead
        if cc + 1 < n_chunks:
            ahead = project(cc + 1)
        finish(cc, t)

    qi = lax.broadcasted_iota(jnp.int32, (rows_stack, LANES), 0) % BLOCK
    kc = lax.broadcasted_iota(jnp.int32, (rows_stack, LANES), 1)
    take_prev = kc > qi
    low_q = kc < HEAD_DIM
    kv_lane = lax.broadcasted_iota(jnp.int32, (2 * BLOCK, LANES), 1)
    k_first = (kv_lane // (HEAD_DIM // 2)) % 2 == 0
    v_first = kv_lane < HEAD_DIM
    no_prev = jnp.where(first, -jnp.inf, 0.0)
    zero_k = jnp.zeros((2 * BLOCK, LANES), BF16)
    one_k = jnp.ones((2 * BLOCK, LANES), BF16)
    zero_p = jnp.zeros((rows_stack, LANES), F32)

    def block_rows(n, count):
        start = n * BLOCK if isinstance(n, int) else pl.multiple_of(n * BLOCK, BLOCK)
        return pl.ds(start, count * BLOCK)

    units = [(sg, half) for sg in range(KV_SLICES) for half in range(2)]

    def score_tile(n, unit):
        sg, half = unit
        kb = k_scr[block_rows(n, 2), sg * LANES:(sg + 1) * LANES]
        kk = jnp.where(k_first, kb, zero_k) if half == 0 else jnp.where(k_first, zero_k, kb)
        sc = lax.dot_general(q_scr[sg, n], kk, (((1,), (1,)), ((), ())),
                             preferred_element_type=F32)
        bias = jnp.where(n == 0, no_prev, 0.0)
        return jnp.where(take_prev, sc[:, :BLOCK] + bias, sc[:, BLOCK:])

    def weighted_values(n, slot, unit):
        sg, half = unit
        s = sc_scr[slot, 2 * sg + half]
        sink = sink_scr[2 * sg + half]
        m = jnp.maximum(jnp.max(s, axis=-1, keepdims=True), sink)
        p = jnp.exp2(s - m)
        p_cat = jnp.concatenate([jnp.where(take_prev, p, zero_p).astype(BF16),
                                 jnp.where(take_prev, zero_p, p).astype(BF16)], axis=1)
        vb = v_scr[block_rows(n, 2), sg * LANES:(sg + 1) * LANES]
        vv = jnp.where(v_first, vb, one_k) if half == 0 else jnp.where(v_first, one_k, vb)
        o = jnp.dot(p_cat, vv, preferred_element_type=F32)
        return o, o + jnp.exp2(sink - m)

    def combine(n, sg, lo, hi):
        (o_lo, d_lo), (o_hi, d_hi) = lo, hi
        numer = jnp.where(low_q, o_lo, o_hi)
        den = pltpu.roll(jnp.where(low_q, d_hi, d_lo), HEAD_DIM, 1)
        out = (numer * (1.0 / den)).astype(BF16)
        for g in range(STACK):
            sl = sg * STACK + g
            att_scr[block_rows(n, 1), sl * LANES:(sl + 1) * LANES] = out[g * BLOCK:(g + 1) * BLOCK]

    def attend(n, slot, ahead_of=None):
        ahead, res = [], []
        for unit in units:
            if ahead_of is not None:
                ahead.append(score_tile(ahead_of, unit))
            res.append(weighted_values(n, slot, unit))
            if unit[1] == 1:
                combine(n, unit[0], *res)
                res = []
        return ahead

    def stage(slot, tiles):
        for u, tile in enumerate(tiles):
            sc_scr[slot, u] = tile

    per_trip = sc_scr.shape[0]
    for slot in range(per_trip):
        stage(slot, [score_tile(slot, unit) for unit in units])

    n_trips = nb // per_trip

    def out_proj(i):
        rows = block_rows(i * per_trip, per_trip)
        y = jnp.dot(att_scr[rows, :], wo_ref[...], preferred_element_type=F32) + bo_ref[...]
        o_ref[rows, :] = x_ref[rows, :] + gate * y

    def trip(i, last=False):
        for slot in range(per_trip):
            n = i * per_trip + slot
            ahead = attend(n, slot, ahead_of=None if last else n + per_trip)
            if not last:
                stage(slot, ahead)

    trip(0)

    def body(i, carry):
        out_proj(i - 1)
        trip(i)
        return carry

    lax.fori_loop(1, n_trips - 1, body, 0)
    out_proj(n_trips - 2)
    trip(n_trips - 1, last=True)
    out_proj(n_trips - 1)

    k_scr[0:BLOCK, :] = k_scr[tq:tq + BLOCK, :]
    v_scr[0:BLOCK, :] = v_scr[tq:tq + BLOCK, :]


def _slice_cols(head_a, head_b):
    half = HEAD_DIM // 2
    cols = []
    for part in range(2):
        for head in (head_a, head_b):
            start = head * HEAD_DIM + part * half
            cols.extend(range(start, start + half))
    return cols


def _attn_layout():
    q_cols, k_cols, heads = [], [], []
    for sl in range(Q_SLICES):
        pair, g = divmod(sl, STACK)
        a, b = (2 * pair) * GROUP + g, (2 * pair + 1) * GROUP + g
        heads.extend([a, b])
        q_cols.extend(_slice_cols(a, b))
    for sg in range(KV_SLICES):
        k_cols.extend(_slice_cols(2 * sg, 2 * sg + 1))
    as_idx = lambda v: np.asarray(v, np.int32)
    return as_idx(q_cols), as_idx(k_cols), as_idx(heads)


def _slice_gain(gain):
    half = HEAD_DIM // 2
    lo, hi = gain[:, :half], gain[:, half:]
    one_slice = jnp.concatenate([lo, lo, hi, hi], axis=1)
    reps = MXU_COLS // LANES
    return jnp.tile(one_slice, (1, reps)).reshape(gain.shape[0], 1, MXU_COLS)


def _attn_call(x, mod, gains, cos_t, sin_t, wqkv, bqkv, qg, kg, sinks, wo, bo, layer, j, seq):
    n_rows, d = x.shape
    tq = ATTN_TILE
    nb = tq // BLOCK
    x_spec, mod_spec = _token_specs(x, mod, layer, tq, seq)
    rope_spec = pl.BlockSpec((tq, LANES), lambda t: (t, 0))
    return pl.pallas_call(
        functools.partial(_attn_kernel, attn_layer=j, per_seq=seq // tq),
        grid=(n_rows // tq,),
        in_specs=[
            x_spec, mod_spec,
            _resident((None, None, 1, d), (layer, 1)),
            rope_spec, rope_spec,
            _resident((None,) + wqkv.shape[1:], (j,)),
            _resident((None,) + bqkv.shape[1:], (j,)),
            _resident((None,) + qg.shape[1:], (j,)),
            _resident((None,) + kg.shape[1:], (j,)),
            pl.BlockSpec(memory_space=pltpu.SMEM),
            _resident((None,) + wo.shape[1:], (j,)),
            _resident((None,) + bo.shape[1:], (j,)),
        ],
        out_specs=x_spec,
        out_shape=jax.ShapeDtypeStruct(x.shape, F32),
        scratch_shapes=[
            pltpu.VMEM((KV_SLICES, nb, STACK * BLOCK, LANES), BF16),
            pltpu.VMEM((BLOCK + tq, KV_WIDTH), BF16),
            pltpu.VMEM((BLOCK + tq, KV_WIDTH), BF16),
            pltpu.VMEM((tq, d), BF16),
            pltpu.VMEM((2 * KV_SLICES, STACK * BLOCK, LANES), F32),
            pltpu.VMEM((ATTN_BLOCKS_PER_TRIP, 2 * KV_SLICES, STACK * BLOCK, BLOCK), F32),
        ],
        compiler_params=_TOKEN_PARAMS,
        name=f"attn_l{layer}",
    )(x, mod, gains, cos_t, sin_t, wqkv, bqkv, qg, kg, sinks, wo, bo)


def _conv_kernel(x_ref, mod_ref, gain_ref, win_ref, cw_ref, wout_ref, o_ref, u_scr, *, per_seq):
    tq, d = x_ref.shape
    pad = u_scr.shape[0] - tq

    @pl.when(pl.program_id(0) % per_seq == 0)
    def _():
        u_scr[0:pad, :] = jnp.zeros((pad, d), F32)

    shift, scale, gate = _mod_rows(mod_ref, 1)
    gain = gain_ref[...]
    for r0 in range(0, tq, CONV_ROWS):
        rows = slice(r0, r0 + CONV_ROWS)
        x = x_ref[rows, :]
        h = _mod_norm(x, gain, shift, scale).astype(BF16)
        gc = jnp.dot(h, win_ref[:, d:2 * d], preferred_element_type=F32)
        v = jnp.dot(h, win_ref[:, 2 * d:3 * d], preferred_element_type=F32)
        gb = jnp.dot(h, win_ref[:, 0:d], preferred_element_type=F32)
        u_scr[pad + r0:pad + r0 + CONV_ROWS, :] = gc * v
        conv = cw_ref[CONV_WIDTH - 1:CONV_WIDTH, :] * u_scr[pad + r0:pad + r0 + CONV_ROWS, :]
        for k in range(1, CONV_WIDTH):
            w = cw_ref[CONV_WIDTH - 1 - k:CONV_WIDTH - k, :]
            conv = conv + w * u_scr[pad + r0 - k:pad + r0 - k + CONV_ROWS, :]
        y = jnp.dot((gb * conv).astype(BF16), wout_ref[...], preferred_element_type=F32)
        o_ref[rows, :] = x + gate * y
    u_scr[0:pad, :] = u_scr[tq:tq + pad, :]


def _conv_call(x, mod, gains, win, cw, wout, layer, j, seq):
    n_rows, d = x.shape
    tq = TOKEN_TILE
    x_spec, mod_spec = _token_specs(x, mod, layer, tq, seq)
    return pl.pallas_call(
        functools.partial(_conv_kernel, per_seq=seq // tq),
        grid=(n_rows // tq,),
        in_specs=[
            x_spec, mod_spec,
            _resident((None, None, 1, d), (layer, 1)),
            _resident(win.shape, ()),
            _resident((None,) + cw.shape[1:], (j,)),
            _resident(wout.shape, ()),
        ],
        out_specs=x_spec,
        out_shape=jax.ShapeDtypeStruct(x.shape, F32),
        scratch_shapes=[pltpu.VMEM((tq + 8, d), F32)],
        compiler_params=_TOKEN_PARAMS,
        name=f"conv_l{layer}",
    )(x, mod, gains, win, cw, wout)


def kernel(x, c, positions, norm_gain, w_ada, b_ada, w_ffn_up, w_ffn_down, attn_w_qkv,
           attn_b_qkv, attn_q_gain, attn_k_gain, attn_sinks, attn_w_o, attn_b_o,
           conv_w_in, conv_w, conv_w_out):
    depth = w_ada.shape[0]
    b, s, d = x.shape
    n_ada = w_ada.shape[2] // d
    mod = _ada_call(c, w_ada, b_ada).reshape(depth, b, n_ada, d)
    cos_t, sin_t = _rope_call(positions)
    gains = norm_gain.reshape(depth, -1, 1, d)

    wup = w_ffn_up[0, 0].astype(BF16)
    wdn = w_ffn_down[0, 0].astype(BF16)

    q_cols, k_cols, heads = _attn_layout()
    n_attn = attn_w_qkv.shape[0]
    qkv_cols = np.concatenate([q_cols, d + k_cols, np.arange(d + KV_WIDTH, d + 2 * KV_WIDTH)])
    chunk = HEAD_DIM // 2
    src_chunks = qkv_cols.reshape(-1, chunk)[:, 0] // chunk
    wqkv = attn_w_qkv.reshape(n_attn, d, -1, chunk)[:, :, src_chunks, :]
    wqkv = wqkv.reshape(n_attn, d, -1).astype(BF16)
    bqkv = attn_b_qkv[:, qkv_cols].reshape(n_attn, 1, -1)
    qg = _slice_gain(attn_q_gain)
    kg = _slice_gain(attn_k_gain)
    sinks = attn_sinks[:, heads]
    wo = attn_w_o.reshape(n_attn, N_HEADS, HEAD_DIM, d)[:, heads].reshape(n_attn, d, d).astype(BF16)
    bo = attn_b_o.reshape(n_attn, 1, d)


    x = x.reshape(b * s, d)
    for i in range(depth):
        j = i // 2
        casts = [(w_ffn_up, (i, 1)), (w_ffn_down, (i, 1))]
        if i % 2 == 1:
            casts += [(conv_w_in, (j,)), (conv_w_out, (j,))]
        x, wup, wdn, *mixer_w = _ffn_call(x, mod, gains, wup, wdn, i, 0, s, casts)
        if i % 2 == 0:
            x = _attn_call(x, mod, gains, cos_t, sin_t, wqkv, bqkv, qg, kg, sinks, wo, bo,
                           i, j, s)
        else:
            x = _conv_call(x, mod, gains, mixer_w[0], conv_w, mixer_w[1], i, j, s)
        casts = [(w_ffn_up, (i + 1, 0)), (w_ffn_down, (i + 1, 0))] if i + 1 < depth else []
        x, *next_w = _ffn_call(x, mod, gains, wup, wdn, i, 2, s, casts)
        if next_w:
            wup, wdn = next_w
    return x.reshape(b, s, d)
```

```python
import functools
import math

import numpy as np
import jax
import jax.numpy as jnp
from jax import lax
from jax.experimental import pallas as pl
from jax.experimental.pallas import tpu as pltpu

F32 = jnp.float32
BF16 = jnp.bfloat16

EPS = 1e-6
ROPE_THETA = 10000.0
N_HEADS = 16
N_KV_HEADS = 4
HEAD_DIM = 64
GROUP = N_HEADS // N_KV_HEADS
BLOCK = 128
CONV_WIDTH = 3
LANES = 128
MXU_COLS = 256
FF_CHUNK = MXU_COLS
TOKEN_TILE = 2048
CONV_ROWS = 512
FFN_TILE = 1024
FFN_ROWS = 512
CAST_ROW_ALIGN = 16
CAST_DOWN_ROWS = 128
ATTN_TILE = 1024
ATTN_BLOCKS_PER_TRIP = 2
ROPE_ROWS = 2048
ADA_COLS = 4608
VMEM_LIMIT = 56 * 1024 * 1024
LOG2E = math.log2(math.e)
KV_WIDTH = N_KV_HEADS * HEAD_DIM
Q_SLICES = N_HEADS * HEAD_DIM // LANES
KV_SLICES = KV_WIDTH // LANES
STACK = Q_SLICES // KV_SLICES


def _resident(shape, index):
    full = tuple(index) + (0,) * (len(shape) - len(index))
    return pl.BlockSpec(shape, lambda *_: full, pipeline_mode=pl.Buffered(1))


def _sigmoid(x):
    return 1.0 / (1.0 + jnp.exp(-x))


def _mod_norm(x, gain, shift, scale):
    ms = jnp.mean(x * x, axis=-1, keepdims=True)
    return (x * lax.rsqrt(ms + EPS)) * (gain * (1.0 + scale)) + shift


def _mod_rows(mod_ref, sub):
    shift = mod_ref[3 * sub + 0:3 * sub + 1, :]
    scale = mod_ref[3 * sub + 1:3 * sub + 2, :]
    gate = mod_ref[3 * sub + 2:3 * sub + 3, :]
    return shift, scale, gate


def _token_specs(x, mod, layer, tile, seq):
    _, d = x.shape
    per_seq = seq // tile
    return (pl.BlockSpec((tile, d), lambda t: (t, 0)),
            pl.BlockSpec((None, None) + mod.shape[2:], lambda t: (layer, t // per_seq, 0, 0)))


_GRID_PARAMS = pltpu.CompilerParams(
    dimension_semantics=("arbitrary", "arbitrary"), vmem_limit_bytes=VMEM_LIMIT)
_TOKEN_PARAMS = pltpu.CompilerParams(
    dimension_semantics=("arbitrary",), vmem_limit_bytes=VMEM_LIMIT)


def _ada_kernel(c_ref, w_ref, b_ref, o_ref):
    c = c_ref[...]
    ca = (c * _sigmoid(c)).astype(BF16)
    o_ref[...] = jnp.dot(ca, w_ref[...].astype(BF16), preferred_element_type=F32) + b_ref[...]


def _ada_call(c, w_ada, b_ada):
    depth, d, n = w_ada.shape
    b = c.shape[0]
    tn = ADA_COLS
    return pl.pallas_call(
        _ada_kernel,
        grid=(depth, n // tn),
        in_specs=[
            pl.BlockSpec((b, d), lambda l, j: (0, 0)),
            pl.BlockSpec((None, d, tn), lambda l, j: (l, 0, j)),
            pl.BlockSpec((None, 1, tn), lambda l, j: (l, 0, j)),
        ],
        out_specs=pl.BlockSpec((None, b, tn), lambda l, j: (l, 0, j)),
        out_shape=jax.ShapeDtypeStruct((depth, b, n), F32),
        compiler_params=_GRID_PARAMS,
        name="ada_mod",
    )(c, w_ada, b_ada.reshape(depth, 1, n))


def _rope_kernel(pos_ref, invf_ref, sign_ref, cos_ref, sin_ref):
    ang = pos_ref[...].astype(F32) * invf_ref[...]
    per_row = LANES // (HEAD_DIM // 2)
    rows = ang.shape[0]
    group = lax.broadcasted_iota(jnp.int32, ang.shape, 1) // (HEAD_DIM // 2)
    for table, out_ref, sign in ((jnp.cos(ang), cos_ref, None), (jnp.sin(ang), sin_ref, sign_ref)):
        for m in range(per_row):
            one = jnp.where(group == m, table, 0.0)
            one = one + pltpu.roll(one, HEAD_DIM // 2, 1)
            one = one + pltpu.roll(one, HEAD_DIM, 1)
            if sign is not None:
                one = one * sign[...]
            out_ref[pl.ds(m, rows, stride=per_row), :] = one


def _rope_call(positions):
    b, s = positions.shape
    half = HEAD_DIM // 2
    per_row = LANES // half
    rows = b * s // per_row
    inv_freq = ROPE_THETA ** (-jnp.arange(0, HEAD_DIM, 2, dtype=F32) / HEAD_DIM)
    invf = jnp.tile(inv_freq, per_row).reshape(1, LANES)
    sign = jnp.where(jnp.arange(LANES) < LANES // 2, -1.0, 1.0).astype(F32).reshape(1, LANES)
    pos_rep = jnp.repeat(positions.reshape(rows, per_row), half, axis=1)
    tr = min(ROPE_ROWS, rows)
    const = pl.BlockSpec((1, LANES), lambda i: (0, 0))
    out_spec = pl.BlockSpec((tr * per_row, LANES), lambda i: (i, 0))
    return pl.pallas_call(
        _rope_kernel,
        grid=(rows // tr,),
        in_specs=[pl.BlockSpec((tr, LANES), lambda i: (i, 0)), const, const],
        out_specs=[out_spec, out_spec],
        out_shape=[jax.ShapeDtypeStruct((b * s, LANES), F32)] * 2,
        compiler_params=pltpu.CompilerParams(dimension_semantics=("arbitrary",)),
        name="rope_tables",
    )(pos_rep, invf, sign)


def _ffn_kernel(x_ref, xn_ref, mod_ref, modn_ref, gain_ref, wup_ref, wdn_ref, *rest,
                sub, n_casts):
    cast_src, (o_ref, *cast_dst), (h_scr, a_scr) = (
        rest[:n_casts], rest[n_casts:2 * n_casts + 1], rest[2 * n_casts + 1:])
    for src, dst in zip(cast_src, cast_dst):
        dst[...] = src[...].astype(BF16)
    shift, scale, gate = _mod_rows(mod_ref, sub)
    shift_n, scale_n, _ = _mod_rows(modn_ref, sub)
    gain = gain_ref[...]
    d_ff = wdn_ref.shape[0]
    n_groups = h_scr.shape[0]
    n_chunks = d_ff // FF_CHUNK
    rows_of = lambda g: slice(g * FFN_ROWS, (g + 1) * FFN_ROWS)

    def prepare(g):
        if g < n_groups:
            h = _mod_norm(x_ref[rows_of(g), :], gain, shift, scale)
        else:
            h = _mod_norm(xn_ref[...], gain, shift_n, scale_n)
        h_scr[g % n_groups] = h.astype(BF16)

    def up_chunk(g, c):
        lo = c * FF_CHUNK
        h = h_scr[g]
        gg = jnp.dot(h, wup_ref[:, lo:lo + FF_CHUNK], preferred_element_type=F32)
        uu = jnp.dot(h, wup_ref[:, d_ff + lo:d_ff + lo + FF_CHUNK], preferred_element_type=F32)
        a_scr[rows_of(g), lo:lo + FF_CHUNK] = ((gg * _sigmoid(gg)) * uu).astype(BF16)

    @pl.when(pl.program_id(0) == 0)
    def _():
        prepare(0)
        up_chunk(0, 0)

    def down(g, cols):
        rows = rows_of(g)
        y = jnp.dot(a_scr[rows, :], wdn_ref[:, cols], preferred_element_type=F32)
        o_ref[rows, cols] = x_ref[rows, cols] + (0.5 * gate[:, cols]) * y

    d = x_ref.shape[1]
    for g in range(n_groups):
        for c in range(1, n_chunks):
            up_chunk(g, c)
        down(g, slice(0, d // 2))
        prepare(g + 1)
        up_chunk((g + 1) % n_groups, 0)
        down(g, slice(d // 2, d))


def _ffn_call(x, mod, gains, wup, wdn, layer, sub, seq, casts=()):
    n_rows, d = x.shape
    ts = FFN_TILE
    d_ff = wdn.shape[0]
    n_tiles = n_rows // ts
    groups = ts // FFN_ROWS
    per_seq = seq // ts
    x_spec, mod_spec = _token_specs(x, mod, layer, ts, seq)
    nxt = lambda t: jnp.minimum(t + 1, n_tiles - 1)
    xn_spec = pl.BlockSpec((FFN_ROWS, d), lambda t: (nxt(t) * groups, 0))
    modn_spec = pl.BlockSpec((None, None) + mod.shape[2:],
                             lambda t: (layer, nxt(t) // per_seq, 0, 0))
    in_specs = [x_spec, xn_spec, mod_spec, modn_spec,
                _resident((None, None, 1, d), (layer, sub)),
                _resident(wup.shape, ()), _resident(wdn.shape, ())]
    operands = [x, x, mod, mod, gains, wup, wdn]
    out_specs = [x_spec]
    out_shape = [jax.ShapeDtypeStruct(x.shape, F32)]
    for stacked, lead in casts:
        n_r, n_c = stacked.shape[-2:]
        rows = n_r // n_tiles if n_r % (n_tiles * CAST_ROW_ALIGN) == 0 else CAST_DOWN_ROWS
        last = n_r // rows - 1
        blk = lambda t, last=last: jnp.minimum(t, last)
        in_specs.append(pl.BlockSpec((None,) * len(lead) + (rows, n_c),
                                     lambda t, lead=lead, blk=blk: lead + (blk(t), 0)))
        operands.append(stacked)
        out_specs.append(pl.BlockSpec((rows, n_c), lambda t, blk=blk: (blk(t), 0)))
        out_shape.append(jax.ShapeDtypeStruct((n_r, n_c), BF16))
    return pl.pallas_call(
        functools.partial(_ffn_kernel, sub=sub, n_casts=len(casts)),
        grid=(n_tiles,),
        in_specs=in_specs,
        out_specs=out_specs,
        out_shape=out_shape,
        scratch_shapes=[pltpu.VMEM((groups, FFN_ROWS, d), BF16), pltpu.VMEM((ts, d_ff), BF16)],
        compiler_params=_TOKEN_PARAMS,
        name=f"ffn_l{layer}_s{sub}",
    )(*operands)


def _norm_rope(t, ms, gain, cos, sin):
    tn = (t * lax.rsqrt(ms + EPS)) * gain
    outs = []
    for j in range(t.shape[1] // LANES):
        v = tn[:, j * LANES:(j + 1) * LANES]
        outs.append(v * cos + pltpu.roll(v, LANES // 2, 1) * sin)
    return outs


def _attn_kernel(x_ref, mod_ref, gain_ref, cos_ref, sin_ref, wqkv_ref,
                 bqkv_ref, qg_ref, kg_ref, sink_ref, wo_ref, bo_ref, o_ref,
                 q_scr, k_scr, v_scr, att_scr, sink_scr, sc_scr, *, attn_layer, per_seq):
    tq, d = x_ref.shape
    nb = tq // BLOCK
    rows_stack = STACK * BLOCK
    first = pl.program_id(0) % per_seq == 0

    @pl.when(first)
    def _():
        k_scr[0:BLOCK, :] = jnp.zeros((BLOCK, KV_WIDTH), BF16)
        v_scr[0:BLOCK, :] = jnp.zeros((BLOCK, KV_WIDTH), BF16)

    @pl.when(pl.program_id(0) == 0)
    def _():
        for sg in range(KV_SLICES):
            for half in range(2):
                for g in range(STACK):
                    val = sink_ref[attn_layer, 2 * (sg * STACK + g) + half] * LOG2E
                    sink_scr[2 * sg + half, g * BLOCK:(g + 1) * BLOCK, :] = jnp.full(
                        (BLOCK, LANES), val, F32)

    shift, scale, gate = _mod_rows(mod_ref, 1)
    h = _mod_norm(x_ref[...], gain_ref[...], shift, scale).astype(BF16)

    cos = cos_ref[...]
    sin = sin_ref[...]

    def head_of(col):
        return (col // LANES) * 2 + (col // (HEAD_DIM // 2)) % 2

    r = head_of(lax.broadcasted_iota(jnp.int32, (MXU_COLS, MXU_COLS), 0))
    c = head_of(lax.broadcasted_iota(jnp.int32, (MXU_COLS, MXU_COLS), 1))
    group_mean = jnp.where(r == c, 1.0 / HEAD_DIM, 0.0).astype(BF16)

    n_chunks = (d + 2 * KV_WIDTH) // MXU_COLS
    n_norm = (d + KV_WIDTH) // MXU_COLS
    qgain = qg_ref[...] * (HEAD_DIM ** -0.5 * LOG2E)

    def project(cc):
        cols = slice(cc * MXU_COLS, (cc + 1) * MXU_COLS)
        return jnp.dot(h, wqkv_ref[:, cols], preferred_element_type=F32) + bqkv_ref[:, cols]

    def finish(cc, t):
        if cc == n_norm:
            v_scr[BLOCK:, :] = t.astype(BF16)
            return
        ms = jnp.dot((t * t).astype(BF16), group_mean, preferred_element_type=F32)
        is_q = cc < d // MXU_COLS
        outs = _norm_rope(t, ms, qgain if is_q else kg_ref[...], cos, sin)
        for j, o in enumerate(outs):
            ob = o.astype(BF16)
            if is_q:
                sg, g = divmod(cc * (MXU_COLS // LANES) + j, STACK)
                for n in range(nb):
                    q_scr[sg, n, g * BLOCK:(g + 1) * BLOCK, :] = ob[n * BLOCK:(n + 1) * BLOCK]
            else:
                k_scr[BLOCK:, j * LANES:(j + 1) * LANES] = ob

    ahead = project(0)
    for cc in range(n_chunks):
        t = ahead
        if cc + 1 < n_chunks:
            ahead = project(cc + 1)
        finish(cc, t)

    qi = lax.broadcasted_iota(jnp.int32, (rows_stack, LANES), 0) % BLOCK
    kc = lax.broadcasted_iota(jnp.int32, (rows_stack, LANES), 1)
    take_prev = kc > qi
    low_q = kc < HEAD_DIM
    kv_lane = lax.broadcasted_iota(jnp.int32, (2 * BLOCK, LANES), 1)
    k_first = (kv_lane // (HEAD_DIM // 2)) % 2 == 0
    v_first = kv_lane < HEAD_DIM
    no_prev = jnp.where(first, -jnp.inf, 0.0)
    zero_k = jnp.zeros((2 * BLOCK, LANES), BF16)
    one_k = jnp.ones((2 * BLOCK, LANES), BF16)
    zero_p = jnp.zeros((rows_stack, LANES), F32)

    def block_rows(n, count):
        start = n * BLOCK if isinstance(n, int) else pl.multiple_of(n * BLOCK, BLOCK)
        return pl.ds(start, count * BLOCK)

    units = [(sg, half) for sg in range(KV_SLICES) for half in range(2)]

    def score_tile(n, unit):
        sg, half = unit
        kb = k_scr[block_rows(n, 2), sg * LANES:(sg + 1) * LANES]
        kk = jnp.where(k_first, kb, zero_k) if half == 0 else jnp.where(k_first, zero_k, kb)
        sc = lax.dot_general(q_scr[sg, n], kk, (((1,), (1,)), ((), ())),
                             preferred_element_type=F32)
        bias = jnp.where(n == 0, no_prev, 0.0)
        return jnp.where(take_prev, sc[:, :BLOCK] + bias, sc[:, BLOCK:])

    def weighted_values(n, slot, unit):
        sg, half = unit
        s = sc_scr[slot, 2 * sg + half]
        sink = sink_scr[2 * sg + half]
        m = jnp.maximum(jnp.max(s, axis=-1, keepdims=True), sink)
        p = jnp.exp2(s - m)
        p_cat = jnp.concatenate([jnp.where(take_prev, p, zero_p).astype(BF16),
                                 jnp.where(take_prev, zero_p, p).astype(BF16)], axis=1)
        vb = v_scr[block_rows(n, 2), sg * LANES:(sg + 1) * LANES]
        vv = jnp.where(v_first, vb, one_k) if half == 0 else jnp.where(v_first, one_k, vb)
        o = jnp.dot(p_cat, vv, preferred_element_type=F32)
        return o, o + jnp.exp2(sink - m)

    def combine(n, sg, lo, hi):
        (o_lo, d_lo), (o_hi, d_hi) = lo, hi
        numer = jnp.where(low_q, o_lo, o_hi)
        den = pltpu.roll(jnp.where(low_q, d_hi, d_lo), HEAD_DIM, 1)
        out = (numer * (1.0 / den)).astype(BF16)
        for g in range(STACK):
            sl = sg * STACK + g
            att_scr[block_rows(n, 1), sl * LANES:(sl + 1) * LANES] = out[g * BLOCK:(g + 1) * BLOCK]

    def attend(n, slot, ahead_of=None):
        ahead, res = [], []
        for unit in units:
            if ahead_of is not None:
                ahead.append(score_tile(ahead_of, unit))
            res.append(weighted_values(n, slot, unit))
            if unit[1] == 1:
                combine(n, unit[0], *res)
                res = []
        return ahead

    def stage(slot, tiles):
        for u, tile in enumerate(tiles):
            sc_scr[slot, u] = tile

    per_trip = sc_scr.shape[0]
    for slot in range(per_trip):
        stage(slot, [score_tile(slot, unit) for unit in units])

    n_trips = nb // per_trip

    def out_proj(i):
        rows = block_rows(i * per_trip, per_trip)
        y = jnp.dot(att_scr[rows, :], wo_ref[...], preferred_element_type=F32) + bo_ref[...]
        o_ref[rows, :] = x_ref[rows, :] + gate * y

    def trip(i, last=False):
        for slot in range(per_trip):
            n = i * per_trip + slot
            ahead = attend(n, slot, ahead_of=None if last else n + per_trip)
            if not last:
                stage(slot, ahead)

    trip(0)

    def body(i, carry):
        out_proj(i - 1)
        trip(i)
        return carry

    lax.fori_loop(1, n_trips - 1, body, 0)
    out_proj(n_trips - 2)
    trip(n_trips - 1, last=True)
    out_proj(n_trips - 1)

    k_scr[0:BLOCK, :] = k_scr[tq:tq + BLOCK, :]
    v_scr[0:BLOCK, :] = v_scr[tq:tq + BLOCK, :]


def _slice_cols(head_a, head_b):
    half = HEAD_DIM // 2
    cols = []
    for part in range(2):
        for head in (head_a, head_b):
            start = head * HEAD_DIM + part * half
            cols.extend(range(start, start + half))
    return cols


def _attn_layout():
    q_cols, k_cols, heads = [], [], []
    for sl in range(Q_SLICES):
        pair, g = divmod(sl, STACK)
        a, b = (2 * pair) * GROUP + g, (2 * pair + 1) * GROUP + g
        heads.extend([a, b])
        q_cols.extend(_slice_cols(a, b))
    for sg in range(KV_SLICES):
        k_cols.extend(_slice_cols(2 * sg, 2 * sg + 1))
    as_idx = lambda v: np.asarray(v, np.int32)
    return as_idx(q_cols), as_idx(k_cols), as_idx(heads)


def _slice_gain(gain):
    half = HEAD_DIM // 2
    lo, hi = gain[:, :half], gain[:, half:]
    one_slice = jnp.concatenate([lo, lo, hi, hi], axis=1)
    reps = MXU_COLS // LANES
    return jnp.tile(one_slice, (1, reps)).reshape(gain.shape[0], 1, MXU_COLS)


def _attn_call(x, mod, gains, cos_t, sin_t, wqkv, bqkv, qg, kg, sinks, wo, bo, layer, j, seq):
    n_rows, d = x.shape
    tq = ATTN_TILE
    nb = tq // BLOCK
    x_spec, mod_spec = _token_specs(x, mod, layer, tq, seq)
    rope_spec = pl.BlockSpec((tq, LANES), lambda t: (t, 0))
    return pl.pallas_call(
        functools.partial(_attn_kernel, attn_layer=j, per_seq=seq // tq),
        grid=(n_rows // tq,),
        in_specs=[
            x_spec, mod_spec,
            _resident((None, None, 1, d), (layer, 1)),
            rope_spec, rope_spec,
            _resident((None,) + wqkv.shape[1:], (j,)),
            _resident((None,) + bqkv.shape[1:], (j,)),
            _resident((None,) + qg.shape[1:], (j,)),
            _resident((None,) + kg.shape[1:], (j,)),
            pl.BlockSpec(memory_space=pltpu.SMEM),
            _resident((None,) + wo.shape[1:], (j,)),
            _resident((None,) + bo.shape[1:], (j,)),
        ],
        out_specs=x_spec,
        out_shape=jax.ShapeDtypeStruct(x.shape, F32),
        scratch_shapes=[
            pltpu.VMEM((KV_SLICES, nb, STACK * BLOCK, LANES), BF16),
            pltpu.VMEM((BLOCK + tq, KV_WIDTH), BF16),
            pltpu.VMEM((BLOCK + tq, KV_WIDTH), BF16),
            pltpu.VMEM((tq, d), BF16),
            pltpu.VMEM((2 * KV_SLICES, STACK * BLOCK, LANES), F32),
            pltpu.VMEM((ATTN_BLOCKS_PER_TRIP, 2 * KV_SLICES, STACK * BLOCK, BLOCK), F32),
        ],
        compiler_params=_TOKEN_PARAMS,
        name=f"attn_l{layer}",
    )(x, mod, gains, cos_t, sin_t, wqkv, bqkv, qg, kg, sinks, wo, bo)


def _conv_kernel(x_ref, mod_ref, gain_ref, win_ref, cw_ref, wout_ref, o_ref, u_scr, *, per_seq):
    tq, d = x_ref.shape
    pad = u_scr.shape[0] - tq

    @pl.when(pl.program_id(0) % per_seq == 0)
    def _():
        u_scr[0:pad, :] = jnp.zeros((pad, d), F32)

    shift, scale, gate = _mod_rows(mod_ref, 1)
    gain = gain_ref[...]
    for r0 in range(0, tq, CONV_ROWS):
        rows = slice(r0, r0 + CONV_ROWS)
        x = x_ref[rows, :]
        h = _mod_norm(x, gain, shift, scale).astype(BF16)
        gc = jnp.dot(h, win_ref[:, d:2 * d], preferred_element_type=F32)
        v = jnp.dot(h, win_ref[:, 2 * d:3 * d], preferred_element_type=F32)
        gb = jnp.dot(h, win_ref[:, 0:d], preferred_element_type=F32)
        u_scr[pad + r0:pad + r0 + CONV_ROWS, :] = gc * v
        conv = cw_ref[CONV_WIDTH - 1:CONV_WIDTH, :] * u_scr[pad + r0:pad + r0 + CONV_ROWS, :]
        for k in range(1, CONV_WIDTH):
            w = cw_ref[CONV_WIDTH - 1 - k:CONV_WIDTH - k, :]
            conv = conv + w * u_scr[pad + r0 - k:pad + r0 - k + CONV_ROWS, :]
        y = jnp.dot((gb * conv).astype(BF16), wout_ref[...], preferred_element_type=F32)
        o_ref[rows, :] = x + gate * y
    u_scr[0:pad, :] = u_scr[tq:tq + pad, :]


def _conv_call(x, mod, gains, win, cw, wout, layer, j, seq):
    n_rows, d = x.shape
    tq = TOKEN_TILE
    x_spec, mod_spec = _token_specs(x, mod, layer, tq, seq)
    return pl.pallas_call(
        functools.partial(_conv_kernel, per_seq=seq // tq),
        grid=(n_rows // tq,),
        in_specs=[
            x_spec, mod_spec,
            _resident((None, None, 1, d), (layer, 1)),
            _resident(win.shape, ()),
            _resident((None,) + cw.shape[1:], (j,)),
            _resident(wout.shape, ()),
        ],
        out_specs=x_spec,
        out_shape=jax.ShapeDtypeStruct(x.shape, F32),
        scratch_shapes=[pltpu.VMEM((tq + 8, d), F32)],
        compiler_params=_TOKEN_PARAMS,
        name=f"conv_l{layer}",
    )(x, mod, gains, win, cw, wout)


def kernel(x, c, positions, norm_gain, w_ada, b_ada, w_ffn_up, w_ffn_down, attn_w_qkv,
           attn_b_qkv, attn_q_gain, attn_k_gain, attn_sinks, attn_w_o, attn_b_o,
           conv_w_in, conv_w, conv_w_out):
    depth = w_ada.shape[0]
    b, s, d = x.shape
    n_ada = w_ada.shape[2] // d
    mod = _ada_call(c, w_ada, b_ada).reshape(depth, b, n_ada, d)
    cos_t, sin_t = _rope_call(positions)
    gains = norm_gain.reshape(depth, -1, 1, d)

    wup = w_ffn_up[0, 0].astype(BF16)
    wdn = w_ffn_down[0, 0].astype(BF16)

    q_cols, k_cols, heads = _attn_layout()
    n_attn = attn_w_qkv.shape[0]
    qkv_cols = np.concatenate([q_cols, d + k_cols, np.arange(d + KV_WIDTH, d + 2 * KV_WIDTH)])
    chunk = HEAD_DIM // 2
    src_chunks = qkv_cols.reshape(-1, chunk)[:, 0] // chunk
    wqkv = attn_w_qkv.astype(BF16).reshape(n_attn, d, -1, chunk)[:, :, src_chunks, :]
    wqkv = wqkv.reshape(n_attn, d, -1)
    bqkv = attn_b_qkv[:, qkv_cols].reshape(n_attn, 1, -1)
    qg = _slice_gain(attn_q_gain)
    kg = _slice_gain(attn_k_gain)
    sinks = attn_sinks[:, heads]
    wo = attn_w_o.astype(BF16).reshape(n_attn, N_HEADS, HEAD_DIM, d)[:, heads].reshape(n_attn, d, d)
    bo = attn_b_o.reshape(n_attn, 1, d)


    x = x.reshape(b * s, d)
    for i in range(depth):
        j = i // 2
        casts = [(w_ffn_up, (i, 1)), (w_ffn_down, (i, 1))]
        if i % 2 == 1:
            casts += [(conv_w_in, (j,)), (conv_w_out, (j,))]
        x, wup, wdn, *mixer_w = _ffn_call(x, mod, gains, wup, wdn, i, 0, s, casts)
        if i % 2 == 0:
            x = _attn_call(x, mod, gains, cos_t, sin_t, wqkv, bqkv, qg, kg, sinks, wo, bo,
                           i, j, s)
        else:
            x = _conv_call(x, mod, gains, mixer_w[0], conv_w, mixer_w[1], i, j, s)
        casts = [(w_ffn_up, (i + 1, 0)), (w_ffn_down, (i + 1, 0))] if i + 1 < depth else []
        x, *next_w = _ffn_call(x, mod, gains, wup, wdn, i, 2, s, casts)
        if next_w:
            wup, wdn = next_w
    return x.reshape(b, s, d)
```

```python
import functools
import math

import numpy as np
import jax
import jax.numpy as jnp
from jax import lax
from jax.experimental import pallas as pl
from jax.experimental.pallas import tpu as pltpu

F32 = jnp.float32
BF16 = jnp.bfloat16

EPS = 1e-6
ROPE_THETA = 10000.0
N_HEADS = 16
N_KV_HEADS = 4
HEAD_DIM = 64
GROUP = N_HEADS // N_KV_HEADS
BLOCK = 128
CONV_WIDTH = 3
LANES = 128
MXU_COLS = 256
FF_CHUNK = MXU_COLS
TOKEN_TILE = 2048
CONV_ROWS = 512
FFN_TILE = 1024
FFN_ROWS = 512
CAST_ROW_ALIGN = 16
CAST_DOWN_ROWS = 128
ATTN_TILE = 1024
ATTN_ROWS = 512
ATTN_BLOCKS_PER_TRIP = 2
ROPE_ROWS = 2048
ADA_COLS = 4608
VMEM_LIMIT = 56 * 1024 * 1024
LOG2E = math.log2(math.e)
KV_WIDTH = N_KV_HEADS * HEAD_DIM
Q_SLICES = N_HEADS * HEAD_DIM // LANES
KV_SLICES = KV_WIDTH // LANES
STACK = Q_SLICES // KV_SLICES


def _resident(shape, index):
    full = tuple(index) + (0,) * (len(shape) - len(index))
    return pl.BlockSpec(shape, lambda *_: full, pipeline_mode=pl.Buffered(1))


def _sigmoid(x):
    return 1.0 / (1.0 + jnp.exp(-x))


def _mod_norm(x, gain, shift, scale):
    ms = jnp.mean(x * x, axis=-1, keepdims=True)
    return (x * lax.rsqrt(ms + EPS)) * (gain * (1.0 + scale)) + shift


def _mod_rows(mod_ref, sub):
    shift = mod_ref[3 * sub + 0:3 * sub + 1, :]
    scale = mod_ref[3 * sub + 1:3 * sub + 2, :]
    gate = mod_ref[3 * sub + 2:3 * sub + 3, :]
    return shift, scale, gate


def _token_specs(x, mod, layer, tile, seq):
    _, d = x.shape
    per_seq = seq // tile
    return (pl.BlockSpec((tile, d), lambda t: (t, 0)),
            pl.BlockSpec((None, None) + mod.shape[2:], lambda t: (layer, t // per_seq, 0, 0)))


_GRID_PARAMS = pltpu.CompilerParams(
    dimension_semantics=("arbitrary", "arbitrary"), vmem_limit_bytes=VMEM_LIMIT)
_TOKEN_PARAMS = pltpu.CompilerParams(
    dimension_semantics=("arbitrary",), vmem_limit_bytes=VMEM_LIMIT)


def _ada_kernel(c_ref, w_ref, b_ref, o_ref):
    c = c_ref[...]
    ca = (c * _sigmoid(c)).astype(BF16)
    o_ref[...] = jnp.dot(ca, w_ref[...].astype(BF16), preferred_element_type=F32) + b_ref[...]


def _ada_call(c, w_ada, b_ada):
    depth, d, n = w_ada.shape
    b = c.shape[0]
    tn = ADA_COLS
    return pl.pallas_call(
        _ada_kernel,
        grid=(depth, n // tn),
        in_specs=[
            pl.BlockSpec((b, d), lambda l, j: (0, 0)),
            pl.BlockSpec((None, d, tn), lambda l, j: (l, 0, j)),
            pl.BlockSpec((None, 1, tn), lambda l, j: (l, 0, j)),
        ],
        out_specs=pl.BlockSpec((None, b, tn), lambda l, j: (l, 0, j)),
        out_shape=jax.ShapeDtypeStruct((depth, b, n), F32),
        compiler_params=_GRID_PARAMS,
        name="ada_mod",
    )(c, w_ada, b_ada.reshape(depth, 1, n))


def _rope_kernel(pos_ref, invf_ref, sign_ref, cos_ref, sin_ref):
    ang = pos_ref[...].astype(F32) * invf_ref[...]
    per_row = LANES // (HEAD_DIM // 2)
    rows = ang.shape[0]
    group = lax.broadcasted_iota(jnp.int32, ang.shape, 1) // (HEAD_DIM // 2)
    for table, out_ref, sign in ((jnp.cos(ang), cos_ref, None), (jnp.sin(ang), sin_ref, sign_ref)):
        for m in range(per_row):
            one = jnp.where(group == m, table, 0.0)
            one = one + pltpu.roll(one, HEAD_DIM // 2, 1)
            one = one + pltpu.roll(one, HEAD_DIM, 1)
            if sign is not None:
                one = one * sign[...]
            out_ref[pl.ds(m, rows, stride=per_row), :] = one


def _rope_call(positions):
    b, s = positions.shape
    half = HEAD_DIM // 2
    per_row = LANES // half
    rows = b * s // per_row
    inv_freq = ROPE_THETA ** (-jnp.arange(0, HEAD_DIM, 2, dtype=F32) / HEAD_DIM)
    invf = jnp.tile(inv_freq, per_row).reshape(1, LANES)
    sign = jnp.where(jnp.arange(LANES) < LANES // 2, -1.0, 1.0).astype(F32).reshape(1, LANES)
    pos_rep = jnp.repeat(positions.reshape(rows, per_row), half, axis=1)
    tr = min(ROPE_ROWS, rows)
    const = pl.BlockSpec((1, LANES), lambda i: (0, 0))
    out_spec = pl.BlockSpec((tr * per_row, LANES), lambda i: (i, 0))
    return pl.pallas_call(
        _rope_kernel,
        grid=(rows // tr,),
        in_specs=[pl.BlockSpec((tr, LANES), lambda i: (i, 0)), const, const],
        out_specs=[out_spec, out_spec],
        out_shape=[jax.ShapeDtypeStruct((b * s, LANES), F32)] * 2,
        compiler_params=pltpu.CompilerParams(dimension_semantics=("arbitrary",)),
        name="rope_tables",
    )(pos_rep, invf, sign)


def _ffn_kernel(x_ref, xn_ref, mod_ref, modn_ref, gain_ref, wup_ref, wdn_ref, *rest,
                sub, n_casts):
    cast_src, (o_ref, *cast_dst), (h_scr, a_scr) = (
        rest[:n_casts], rest[n_casts:2 * n_casts + 1], rest[2 * n_casts + 1:])
    for src, dst in zip(cast_src, cast_dst):
        dst[...] = src[...].astype(BF16)
    shift, scale, gate = _mod_rows(mod_ref, sub)
    shift_n, scale_n, _ = _mod_rows(modn_ref, sub)
    gain = gain_ref[...]
    d_ff = wdn_ref.shape[0]
    n_groups = h_scr.shape[0]
    n_chunks = d_ff // FF_CHUNK
    rows_of = lambda g: slice(g * FFN_ROWS, (g + 1) * FFN_ROWS)

    def prepare(g):
        if g < n_groups:
            h = _mod_norm(x_ref[rows_of(g), :], gain, shift, scale)
        else:
            h = _mod_norm(xn_ref[...], gain, shift_n, scale_n)
        h_scr[g % n_groups] = h.astype(BF16)

    def up_chunk(g, c):
        lo = c * FF_CHUNK
        h = h_scr[g]
        gg = jnp.dot(h, wup_ref[:, lo:lo + FF_CHUNK], preferred_element_type=F32)
        uu = jnp.dot(h, wup_ref[:, d_ff + lo:d_ff + lo + FF_CHUNK], preferred_element_type=F32)
        a_scr[rows_of(g), lo:lo + FF_CHUNK] = ((gg * _sigmoid(gg)) * uu).astype(BF16)

    @pl.when(pl.program_id(0) == 0)
    def _():
        prepare(0)
        up_chunk(0, 0)

    def down(g, cols):
        rows = rows_of(g)
        y = jnp.dot(a_scr[rows, :], wdn_ref[:, cols], preferred_element_type=F32)
        o_ref[rows, cols] = x_ref[rows, cols] + (0.5 * gate[:, cols]) * y

    d = x_ref.shape[1]
    for g in range(n_groups):
        for c in range(1, n_chunks):
            up_chunk(g, c)
        down(g, slice(0, d // 2))
        prepare(g + 1)
        up_chunk((g + 1) % n_groups, 0)
        down(g, slice(d // 2, d))


def _ffn_call(x, mod, gains, wup, wdn, layer, sub, seq, casts=()):
    n_rows, d = x.shape
    ts = FFN_TILE
    d_ff = wdn.shape[0]
    n_tiles = n_rows // ts
    groups = ts // FFN_ROWS
    per_seq = seq // ts
    x_spec, mod_spec = _token_specs(x, mod, layer, ts, seq)
    nxt = lambda t: jnp.minimum(t + 1, n_tiles - 1)
    xn_spec = pl.BlockSpec((FFN_ROWS, d), lambda t: (nxt(t) * groups, 0))
    modn_spec = pl.BlockSpec((None, None) + mod.shape[2:],
                             lambda t: (layer, nxt(t) // per_seq, 0, 0))
    in_specs = [x_spec, xn_spec, mod_spec, modn_spec,
                _resident((None, None, 1, d), (layer, sub)),
                _resident(wup.shape, ()), _resident(wdn.shape, ())]
    operands = [x, x, mod, mod, gains, wup, wdn]
    out_specs = [x_spec]
    out_shape = [jax.ShapeDtypeStruct(x.shape, F32)]
    for stacked, lead in casts:
        n_r, n_c = stacked.shape[-2:]
        rows = n_r // n_tiles if n_r % (n_tiles * CAST_ROW_ALIGN) == 0 else CAST_DOWN_ROWS
        last = n_r // rows - 1
        blk = lambda t, last=last: jnp.minimum(t, last)
        in_specs.append(pl.BlockSpec((None,) * len(lead) + (rows, n_c),
                                     lambda t, lead=lead, blk=blk: lead + (blk(t), 0)))
        operands.append(stacked)
        out_specs.append(pl.BlockSpec((rows, n_c), lambda t, blk=blk: (blk(t), 0)))
        out_shape.append(jax.ShapeDtypeStruct((n_r, n_c), BF16))
    return pl.pallas_call(
        functools.partial(_ffn_kernel, sub=sub, n_casts=len(casts)),
        grid=(n_tiles,),
        in_specs=in_specs,
        out_specs=out_specs,
        out_shape=out_shape,
        scratch_shapes=[pltpu.VMEM((groups, FFN_ROWS, d), BF16), pltpu.VMEM((ts, d_ff), BF16)],
        compiler_params=_TOKEN_PARAMS,
        name=f"ffn_l{layer}_s{sub}",
    )(*operands)


def _norm_rope(t, ms, gain, cos, sin):
    tn = (t * lax.rsqrt(ms + EPS)) * gain
    outs = []
    for j in range(t.shape[1] // LANES):
        v = tn[:, j * LANES:(j + 1) * LANES]
        outs.append(v * cos + pltpu.roll(v, LANES // 2, 1) * sin)
    return outs


def _attn_kernel(x_ref, mod_ref, gain_ref, cos_ref, sin_ref, wqkv_ref,
                 bqkv_ref, qg_ref, kg_ref, sink_ref, wo_ref, bo_ref, o_ref,
                 q_scr, k_scr, v_scr, att_scr, sink_scr, sc_scr, *, attn_layer, per_seq):
    tq, d = x_ref.shape
    nb = tq // BLOCK
    rows_stack = STACK * BLOCK
    first = pl.program_id(0) % per_seq == 0

    @pl.when(first)
    def _():
        k_scr[0:BLOCK, :] = jnp.zeros((BLOCK, KV_WIDTH), BF16)
        v_scr[0:BLOCK, :] = jnp.zeros((BLOCK, KV_WIDTH), BF16)

    @pl.when(pl.program_id(0) == 0)
    def _():
        for sg in range(KV_SLICES):
            for half in range(2):
                for g in range(STACK):
                    val = sink_ref[attn_layer, 2 * (sg * STACK + g) + half] * LOG2E
                    sink_scr[2 * sg + half, g * BLOCK:(g + 1) * BLOCK, :] = jnp.full(
                        (BLOCK, LANES), val, F32)

    shift, scale, gate = _mod_rows(mod_ref, 1)
    gain = gain_ref[...]

    def head_of(col):
        return (col // LANES) * 2 + (col // (HEAD_DIM // 2)) % 2

    r = head_of(lax.broadcasted_iota(jnp.int32, (MXU_COLS, MXU_COLS), 0))
    c = head_of(lax.broadcasted_iota(jnp.int32, (MXU_COLS, MXU_COLS), 1))
    group_mean = jnp.where(r == c, 1.0 / HEAD_DIM, 0.0).astype(BF16)

    n_chunks = (d + 2 * KV_WIDTH) // MXU_COLS
    n_norm = (d + KV_WIDTH) // MXU_COLS
    qgain = qg_ref[...] * (HEAD_DIM ** -0.5 * LOG2E)

    group_blocks = ATTN_ROWS // BLOCK
    rows_of = lambda grp: slice(grp * ATTN_ROWS, (grp + 1) * ATTN_ROWS)
    normed = {}

    def h_of(grp):
        if grp not in normed:
            normed[grp] = _mod_norm(x_ref[rows_of(grp), :], gain, shift, scale).astype(BF16)
        return normed[grp]

    def project(grp, cc):
        cols = slice(cc * MXU_COLS, (cc + 1) * MXU_COLS)
        return (jnp.dot(h_of(grp), wqkv_ref[:, cols], preferred_element_type=F32)
                + bqkv_ref[:, cols])

    def finish(grp, cc, t):
        r0 = BLOCK + grp * ATTN_ROWS
        if cc == n_norm:
            v_scr[r0:r0 + ATTN_ROWS, :] = t.astype(BF16)
            return
        ms = jnp.dot((t * t).astype(BF16), group_mean, preferred_element_type=F32)
        is_q = cc < d // MXU_COLS
        outs = _norm_rope(t, ms, qgain if is_q else kg_ref[...],
                          cos_ref[rows_of(grp), :], sin_ref[rows_of(grp), :])
        for j, o in enumerate(outs):
            ob = o.astype(BF16)
            if is_q:
                sg, g = divmod(cc * (MXU_COLS // LANES) + j, STACK)
                for nn in range(group_blocks):
                    q_scr[sg, grp * group_blocks + nn, g * BLOCK:(g + 1) * BLOCK, :] = (
                        ob[nn * BLOCK:(nn + 1) * BLOCK])
            else:
                k_scr[r0:r0 + ATTN_ROWS, j * LANES:(j + 1) * LANES] = ob

    tasks = [(grp, cc) for grp in range(tq // ATTN_ROWS) for cc in range(n_chunks)]
    ahead = project(*tasks[0])
    for i, (grp, cc) in enumerate(tasks):
        t = ahead
        if i + 1 < len(tasks):
            ahead = project(*tasks[i + 1])
        finish(grp, cc, t)

    qi = lax.broadcasted_iota(jnp.int32, (rows_stack, LANES), 0) % BLOCK
    kc = lax.broadcasted_iota(jnp.int32, (rows_stack, LANES), 1)
    take_prev = kc > qi
    low_q = kc < HEAD_DIM
    kv_lane = lax.broadcasted_iota(jnp.int32, (2 * BLOCK, LANES), 1)
    k_first = (kv_lane // (HEAD_DIM // 2)) % 2 == 0
    v_first = kv_lane < HEAD_DIM
    no_prev = jnp.where(first, -jnp.inf, 0.0)
    zero_k = jnp.zeros((2 * BLOCK, LANES), BF16)
    one_k = jnp.ones((2 * BLOCK, LANES), BF16)
    zero_p = jnp.zeros((rows_stack, LANES), F32)

    def block_rows(n, count):
        start = n * BLOCK if isinstance(n, int) else pl.multiple_of(n * BLOCK, BLOCK)
        return pl.ds(start, count * BLOCK)

    units = [(sg, half) for sg in range(KV_SLICES) for half in range(2)]

    def score_tile(n, unit):
        sg, half = unit
        kb = k_scr[block_rows(n, 2), sg * LANES:(sg + 1) * LANES]
        kk = jnp.where(k_first, kb, zero_k) if half == 0 else jnp.where(k_first, zero_k, kb)
        sc = lax.dot_general(q_scr[sg, n], kk, (((1,), (1,)), ((), ())),
                             preferred_element_type=F32)
        bias = jnp.where(n == 0, no_prev, 0.0)
        return jnp.where(take_prev, sc[:, :BLOCK] + bias, sc[:, BLOCK:])

    def weighted_values(n, slot, unit):
        sg, half = unit
        s = sc_scr[slot, 2 * sg + half]
        sink = sink_scr[2 * sg + half]
        m = jnp.maximum(jnp.max(s, axis=-1, keepdims=True), sink)
        p = jnp.exp2(s - m)
        p_cat = jnp.concatenate([jnp.where(take_prev, p, zero_p).astype(BF16),
                                 jnp.where(take_prev, zero_p, p).astype(BF16)], axis=1)
        vb = v_scr[block_rows(n, 2), sg * LANES:(sg + 1) * LANES]
        vv = jnp.where(v_first, vb, one_k) if half == 0 else jnp.where(v_first, one_k, vb)
        o = jnp.dot(p_cat, vv, preferred_element_type=F32)
        return o, o + jnp.exp2(sink - m)

    def combine(n, sg, lo, hi):
        (o_lo, d_lo), (o_hi, d_hi) = lo, hi
        numer = jnp.where(low_q, o_lo, o_hi)
        den = pltpu.roll(jnp.where(low_q, d_hi, d_lo), HEAD_DIM, 1)
        out = (numer * (1.0 / den)).astype(BF16)
        for g in range(STACK):
            sl = sg * STACK + g
            att_scr[block_rows(n, 1), sl * LANES:(sl + 1) * LANES] = out[g * BLOCK:(g + 1) * BLOCK]

    def attend(n, slot, ahead_of=None):
        ahead, res = [], []
        for unit in units:
            if ahead_of is not None:
                ahead.append(score_tile(ahead_of, unit))
            res.append(weighted_values(n, slot, unit))
            if unit[1] == 1:
                combine(n, unit[0], *res)
                res = []
        return ahead

    def stage(slot, tiles):
        for u, tile in enumerate(tiles):
            sc_scr[slot, u] = tile

    per_trip = sc_scr.shape[0]
    for slot in range(per_trip):
        stage(slot, [score_tile(slot, unit) for unit in units])

    n_trips = nb // per_trip

    def out_proj(i):
        rows = block_rows(i * per_trip, per_trip)
        y = jnp.dot(att_scr[rows, :], wo_ref[...], preferred_element_type=F32) + bo_ref[...]
        o_ref[rows, :] = x_ref[rows, :] + gate * y

    def trip(i, last=False):
        for slot in range(per_trip):
            n = i * per_trip + slot
            ahead = attend(n, slot, ahead_of=None if last else n + per_trip)
            if not last:
                stage(slot, ahead)

    trip(0)

    def body(i, carry):
        out_proj(i - 1)
        trip(i)
        return carry

    lax.fori_loop(1, n_trips - 1, body, 0)
    out_proj(n_trips - 2)
    trip(n_trips - 1, last=True)
    out_proj(n_trips - 1)

    k_scr[0:BLOCK, :] = k_scr[tq:tq + BLOCK, :]
    v_scr[0:BLOCK, :] = v_scr[tq:tq + BLOCK, :]


def _slice_cols(head_a, head_b):
    half = HEAD_DIM // 2
    cols = []
    for part in range(2):
        for head in (head_a, head_b):
            start = head * HEAD_DIM + part * half
            cols.extend(range(start, start + half))
    return cols


def _attn_layout():
    q_cols, k_cols, heads = [], [], []
    for sl in range(Q_SLICES):
        pair, g = divmod(sl, STACK)
        a, b = (2 * pair) * GROUP + g, (2 * pair + 1) * GROUP + g
        heads.extend([a, b])
        q_cols.extend(_slice_cols(a, b))
    for sg in range(KV_SLICES):
        k_cols.extend(_slice_cols(2 * sg, 2 * sg + 1))
    as_idx = lambda v: np.asarray(v, np.int32)
    return as_idx(q_cols), as_idx(k_cols), as_idx(heads)


def _slice_gain(gain):
    half = HEAD_DIM // 2
    lo, hi = gain[:, :half], gain[:, half:]
    one_slice = jnp.concatenate([lo, lo, hi, hi], axis=1)
    reps = MXU_COLS // LANES
    return jnp.tile(one_slice, (1, reps)).reshape(gain.shape[0], 1, MXU_COLS)


def _attn_call(x, mod, gains, cos_t, sin_t, wqkv, bqkv, qg, kg, sinks, wo, bo, layer, j, seq):
    n_rows, d = x.shape
    tq = ATTN_TILE
    nb = tq // BLOCK
    x_spec, mod_spec = _token_specs(x, mod, layer, tq, seq)
    rope_spec = pl.BlockSpec((tq, LANES), lambda t: (t, 0))
    return pl.pallas_call(
        functools.partial(_attn_kernel, attn_layer=j, per_seq=seq // tq),
        grid=(n_rows // tq,),
        in_specs=[
            x_spec, mod_spec,
            _resident((None, None, 1, d), (layer, 1)),
            rope_spec, rope_spec,
            _resident((None,) + wqkv.shape[1:], (j,)),
            _resident((None,) + bqkv.shape[1:], (j,)),
            _resident((None,) + qg.shape[1:], (j,)),
            _resident((None,) + kg.shape[1:], (j,)),
            pl.BlockSpec(memory_space=pltpu.SMEM),
            _resident((None,) + wo.shape[1:], (j,)),
            _resident((None,) + bo.shape[1:], (j,)),
        ],
        out_specs=x_spec,
        out_shape=jax.ShapeDtypeStruct(x.shape, F32),
        scratch_shapes=[
            pltpu.VMEM((KV_SLICES, nb, STACK * BLOCK, LANES), BF16),
            pltpu.VMEM((BLOCK + tq, KV_WIDTH), BF16),
            pltpu.VMEM((BLOCK + tq, KV_WIDTH), BF16),
            pltpu.VMEM((tq, d), BF16),
            pltpu.VMEM((2 * KV_SLICES, STACK * BLOCK, LANES), F32),
            pltpu.VMEM((ATTN_BLOCKS_PER_TRIP, 2 * KV_SLICES, STACK * BLOCK, BLOCK), F32),
        ],
        compiler_params=_TOKEN_PARAMS,
        name=f"attn_l{layer}",
    )(x, mod, gains, cos_t, sin_t, wqkv, bqkv, qg, kg, sinks, wo, bo)


def _conv_kernel(x_ref, mod_ref, gain_ref, win_ref, cw_ref, wout_ref, o_ref, u_scr, *, per_seq):
    tq, d = x_ref.shape
    pad = u_scr.shape[0] - tq

    @pl.when(pl.program_id(0) % per_seq == 0)
    def _():
        u_scr[0:pad, :] = jnp.zeros((pad, d), F32)

    shift, scale, gate = _mod_rows(mod_ref, 1)
    gain = gain_ref[...]
    for r0 in range(0, tq, CONV_ROWS):
        rows = slice(r0, r0 + CONV_ROWS)
        x = x_ref[rows, :]
        h = _mod_norm(x, gain, shift, scale).astype(BF16)
        gc = jnp.dot(h, win_ref[:, d:2 * d], preferred_element_type=F32)
        v = jnp.dot(h, win_ref[:, 2 * d:3 * d], preferred_element_type=F32)
        gb = jnp.dot(h, win_ref[:, 0:d], preferred_element_type=F32)
        u_scr[pad + r0:pad + r0 + CONV_ROWS, :] = gc * v
        conv = cw_ref[CONV_WIDTH - 1:CONV_WIDTH, :] * u_scr[pad + r0:pad + r0 + CONV_ROWS, :]
        for k in range(1, CONV_WIDTH):
            w = cw_ref[CONV_WIDTH - 1 - k:CONV_WIDTH - k, :]
            conv = conv + w * u_scr[pad + r0 - k:pad + r0 - k + CONV_ROWS, :]
        y = jnp.dot((gb * conv).astype(BF16), wout_ref[...], preferred_element_type=F32)
        o_ref[rows, :] = x + gate * y
    u_scr[0:pad, :] = u_scr[tq:tq + pad, :]


def _conv_call(x, mod, gains, win, cw, wout, layer, j, seq):
    n_rows, d = x.shape
    tq = TOKEN_TILE
    x_spec, mod_spec = _token_specs(x, mod, layer, tq, seq)
    return pl.pallas_call(
        functools.partial(_conv_kernel, per_seq=seq // tq),
        grid=(n_rows // tq,),
        in_specs=[
            x_spec, mod_spec,
            _resident((None, None, 1, d), (layer, 1)),
            _resident(win.shape, ()),
            _resident((None,) + cw.shape[1:], (j,)),
            _resident(wout.shape, ()),
        ],
        out_specs=x_spec,
        out_shape=jax.ShapeDtypeStruct(x.shape, F32),
        scratch_shapes=[pltpu.VMEM((tq + 8, d), F32)],
        compiler_params=_TOKEN_PARAMS,
        name=f"conv_l{layer}",
    )(x, mod, gains, win, cw, wout)


def kernel(x, c, positions, norm_gain, w_ada, b_ada, w_ffn_up, w_ffn_down, attn_w_qkv,
           attn_b_qkv, attn_q_gain, attn_k_gain, attn_sinks, attn_w_o, attn_b_o,
           conv_w_in, conv_w, conv_w_out):
    depth = w_ada.shape[0]
    b, s, d = x.shape
    n_ada = w_ada.shape[2] // d
    mod = _ada_call(c, w_ada, b_ada).reshape(depth, b, n_ada, d)
    cos_t, sin_t = _rope_call(positions)
    gains = norm_gain.reshape(depth, -1, 1, d)

    wup = w_ffn_up[0, 0].astype(BF16)
    wdn = w_ffn_down[0, 0].astype(BF16)

    q_cols, k_cols, heads = _attn_layout()
    n_attn = attn_w_qkv.shape[0]
    qkv_cols = np.concatenate([q_cols, d + k_cols, np.arange(d + KV_WIDTH, d + 2 * KV_WIDTH)])
    chunk = HEAD_DIM // 2
    src_chunks = qkv_cols.reshape(-1, chunk)[:, 0] // chunk
    wqkv = attn_w_qkv.astype(BF16).reshape(n_attn, d, -1, chunk)[:, :, src_chunks, :]
    wqkv = wqkv.reshape(n_attn, d, -1)
    bqkv = attn_b_qkv[:, qkv_cols].reshape(n_attn, 1, -1)
    qg = _slice_gain(attn_q_gain)
    kg = _slice_gain(attn_k_gain)
    sinks = attn_sinks[:, heads]
    wo = attn_w_o.astype(BF16).reshape(n_attn, N_HEADS, HEAD_DIM, d)[:, heads].reshape(n_attn, d, d)
    bo = attn_b_o.reshape(n_attn, 1, d)


    x = x.reshape(b * s, d)
    for i in range(depth):
        j = i // 2
        casts = [(w_ffn_up, (i, 1)), (w_ffn_down, (i, 1))]
        if i % 2 == 1:
            casts += [(conv_w_in, (j,)), (conv_w_out, (j,))]
        x, wup, wdn, *mixer_w = _ffn_call(x, mod, gains, wup, wdn, i, 0, s, casts)
        if i % 2 == 0:
            x = _attn_call(x, mod, gains, cos_t, sin_t, wqkv, bqkv, qg, kg, sinks, wo, bo,
                           i, j, s)
        else:
            x = _conv_call(x, mod, gains, mixer_w[0], conv_w, mixer_w[1], i, j, s)
        casts = [(w_ffn_up, (i + 1, 0)), (w_ffn_down, (i + 1, 0))] if i + 1 < depth else []
        x, *next_w = _ffn_call(x, mod, gains, wup, wdn, i, 2, s, casts)
        if next_w:
            wup, wdn = next_w
    return x.reshape(b, s, d)
```
